```python
import jax, jax.numpy as jnp
from jax import lax
import numpy as np

D_MODEL = 1024
BATCH = 8
SEQ = 4096
DEPTH = 4

N_META = 16
EPS = 1e-6
D_FF = 2816
FFN_HALF = 0.5

A_WIDTH = 3 * D_MODEL // 8
A_DK = 64
A_DV = 64
A_HEADS = A_WIDTH // A_DV
A_CHUNK = 64

B_HEAD_DIM = 64
B_WIDTH = 3 * D_MODEL // 8
B_Q_HEADS = B_WIDTH // B_HEAD_DIM
B_KV_HEADS = 2
B_WINDOW = 128
B_BLOCK = 128

C_CHANNELS = D_MODEL - A_WIDTH - B_WIDTH
C_CONV_WIDTH = 31

MIX_WIDTH = A_WIDTH + B_WIDTH + C_CHANNELS
IN_SPLITS = (A_HEADS * A_DK, A_HEADS * A_DK, A_WIDTH, A_WIDTH,
             B_Q_HEADS * B_HEAD_DIM, B_KV_HEADS * B_HEAD_DIM, B_KV_HEADS * B_HEAD_DIM,
             2 * C_CHANNELS)
IN_WIDTH = sum(IN_SPLITS)

kernel_name = 'hymba_style_hgrn2_swa_conformer_macaron'


def rms_norm(x, g):
    xf = x.astype(jnp.float32)
    y = xf * lax.rsqrt(jnp.mean(xf * xf, axis=-1, keepdims=True) + EPS)
    return (y * g.astype(jnp.float32)).astype(x.dtype)


def swiglu(x, w_gate, w_up, w_down):
    return (jax.nn.silu(x @ w_gate) * (x @ w_up)) @ w_down


def pad_front(t, n):
    return jnp.pad(t, ((0, 0), (n, 0), (0, 0)))


def hgrn2_recurrence(q, f_pre, i, g, lb, out_gain):
    Bsz, L, _ = q.shape
    dt = q.dtype
    pad = A_CHUNK - N_META
    total = L + pad
    nc = total // A_CHUNK
    z = f_pre.astype(jnp.float32)
    lb = lb.astype(jnp.float32)
    log_f = jnp.logaddexp(jnp.log(lb), jnp.log1p(-lb) + jax.nn.log_sigmoid(z))
    k = (1.0 - lb) * jax.nn.sigmoid(-z)

    def chunks(t, d):
        return pad_front(t, pad).reshape(Bsz, nc, A_CHUNK, A_HEADS, d).transpose(1, 0, 3, 2, 4)

    qc = chunks(q.astype(jnp.float32), A_DK)
    kc = chunks(k, A_DK)
    gc = chunks(log_f, A_DK)
    vc = chunks(i.astype(jnp.float32), A_DV)
    causal = jnp.tril(jnp.ones((A_CHUNK, A_CHUNK), dtype=bool))

    def step(S, inp):
        qb, kb, vb, gb = inp
        G = jnp.cumsum(gb, axis=2)
        o_inter = jnp.einsum('bhtk,bhkv->bhtv', qb * jnp.exp(G), S)
        diff = G[:, :, :, None, :] - G[:, :, None, :, :]
        decay = jnp.exp(jnp.where(causal[:, :, None], diff, -jnp.inf))
        A = jnp.einsum('bhtsk,bhsk->bhts', decay * qb[:, :, :, None, :], kb)
        o_intra = jnp.einsum('bhts,bhsv->bhtv', A, vb)
        G_last = G[:, :, -1:, :]
        S_new = (jnp.exp(G_last[:, :, 0, :])[..., None] * S
                 + jnp.einsum('bhsk,bhsv->bhkv', kb * jnp.exp(G_last - G), vb))
        return S_new, o_inter + o_intra

    S0 = jnp.zeros((Bsz, A_HEADS, A_DK, A_DV), jnp.float32)
    _, o = lax.scan(step, S0, (qc, kc, vc, gc))
    o = o.transpose(1, 0, 3, 2, 4).reshape(Bsz, total, A_HEADS, A_DV)[:, pad:]
    o = rms_norm(o, out_gain)
    gate = jax.nn.silu(g.astype(jnp.float32)).reshape(Bsz, L, A_HEADS, A_DV)
    return (o * gate).reshape(Bsz, L, A_WIDTH).astype(dt)


def sliding_window_sink_attention(q, k, v, sinks):
    Bsz, L, _ = q.shape
    dt = q.dtype
    G = B_Q_HEADS // B_KV_HEADS
    pad = B_BLOCK - N_META
    total = L + pad
    nb = total // B_BLOCK
    qb = pad_front(q, pad).reshape(Bsz, nb, B_BLOCK, B_KV_HEADS, G, B_HEAD_DIM)
    kb = pad_front(k, pad).reshape(Bsz, nb, B_BLOCK, B_KV_HEADS, B_HEAD_DIM)
    vb = pad_front(v, pad).reshape(Bsz, nb, B_BLOCK, B_KV_HEADS, B_HEAD_DIM)

    def band(t):
        prev = jnp.concatenate([jnp.zeros_like(t[:, :1]), t[:, :-1]], axis=1)
        return jnp.concatenate([prev, t], axis=2)

    k_band, v_band = band(kb), band(vb)
    k_meta = k[:, :N_META].reshape(Bsz, N_META, B_KV_HEADS, B_HEAD_DIM)
    v_meta = v[:, :N_META].reshape(Bsz, N_META, B_KV_HEADS, B_HEAD_DIM)
    scale = B_HEAD_DIM ** -0.5
    s_band = jnp.einsum('bnqhgd,bnkhd->bnhgqk', qb, k_band)
    s_meta = jnp.einsum('bnqhgd,bmhd->bnhgqm', qb, k_meta)
    scores = jnp.concatenate([s_band, s_meta], axis=-1).astype(jnp.float32) * scale

    q_pos = jnp.arange(total).reshape(nb, B_BLOCK)
    k_pos = (jnp.arange(nb)[:, None] - 1) * B_BLOCK + jnp.arange(2 * B_BLOCK)[None, :]
    qp, kp = q_pos[:, :, None], k_pos[:, None, :]
    band_mask = (kp >= pad + N_META) & (kp <= qp) & (qp - kp < B_WINDOW)
    meta_mask = (pad + jnp.arange(N_META))[None, None, :] <= qp
    mask = jnp.concatenate([band_mask, meta_mask], axis=-1)[None, :, None, None]
    scores = jnp.where(mask, scores, -jnp.inf)
    sink = jnp.broadcast_to(sinks.astype(jnp.float32).reshape(1, 1, B_KV_HEADS, G, 1, 1),
                            scores.shape[:-1] + (1,))
    probs = jax.nn.softmax(jnp.concatenate([scores, sink], axis=-1), axis=-1)[..., :-1].astype(dt)
    out = (jnp.einsum('bnhgqk,bnkhd->bnqhgd', probs[..., :2 * B_BLOCK], v_band)
           + jnp.einsum('bnhgqm,bmhd->bnqhgd', probs[..., 2 * B_BLOCK:], v_meta))
    return out.reshape(Bsz, total, B_Q_HEADS * B_HEAD_DIM)[:, pad:]


def conformer_conv(u, dw_w, dw_b, ln_g, ln_b):
    a, b = jnp.split(u, 2, axis=-1)
    h = a * jax.nn.sigmoid(b)
    h = lax.conv_general_dilated(h, dw_w[:, None, :].astype(h.dtype), (1,),
                                 [(C_CONV_WIDTH - 1, 0)],
                                 dimension_numbers=('NWC', 'WIO', 'NWC'),
                                 feature_group_count=C_CHANNELS) + dw_b
    hf = h.astype(jnp.float32)
    mu = jnp.mean(hf, axis=-1, keepdims=True)
    var = jnp.mean(jnp.square(hf - mu), axis=-1, keepdims=True)
    hn = (hf - mu) * lax.rsqrt(var + EPS) * ln_g.astype(jnp.float32) + ln_b.astype(jnp.float32)
    return jax.nn.silu(hn).astype(u.dtype)


def setup_inputs(seed: int = 0) -> dict:
    key = jax.random.key(seed)
    ks = jax.random.split(key, 24)
    f32 = jnp.float32

    def nrm(k, shape, scale):
        return jax.random.normal(k, shape, f32) * scale

    def gain(k, shape):
        return 1.0 + 0.02 * jax.random.normal(k, shape, f32)

    return {
        'x': nrm(ks[0], (BATCH, SEQ, D_MODEL), 1.0),
        'meta_tokens': nrm(ks[1], (N_META, D_MODEL), 1.0),
        'ffn1_norm': gain(ks[2], (DEPTH, D_MODEL)),
        'ffn1_w_gate': nrm(ks[3], (DEPTH, D_MODEL, D_FF), D_MODEL ** -0.5),
        'ffn1_w_up': nrm(ks[4], (DEPTH, D_MODEL, D_FF), D_MODEL ** -0.5),
        'ffn1_w_down': nrm(ks[5], (DEPTH, D_FF, D_MODEL), D_FF ** -0.5),
        'mix_norm': gain(ks[6], (DEPTH, D_MODEL)),
        'w_in': nrm(ks[7], (DEPTH, D_MODEL, IN_WIDTH), D_MODEL ** -0.5),
        'w_out': nrm(ks[8], (DEPTH, MIX_WIDTH, D_MODEL), MIX_WIDTH ** -0.5),
        'hgrn_lb_logits': nrm(ks[9], (DEPTH, A_HEADS * A_DK), 0.5),
        'hgrn_out_norm': gain(ks[10], (DEPTH, A_DV)),
        'attn_sinks': nrm(ks[11], (DEPTH, B_Q_HEADS), 0.5),
        'conv_dw_w': nrm(ks[12], (DEPTH, C_CONV_WIDTH, C_CHANNELS), C_CONV_WIDTH ** -0.5),
        'conv_dw_b': nrm(ks[13], (DEPTH, C_CHANNELS), 0.02),
        'conv_ln_g': gain(ks[14], (DEPTH, C_CHANNELS)),
        'conv_ln_b': nrm(ks[15], (DEPTH, C_CHANNELS), 0.02),
        'ffn2_norm': gain(ks[16], (DEPTH, D_MODEL)),
        'ffn2_w_gate': nrm(ks[17], (DEPTH, D_MODEL, D_FF), D_MODEL ** -0.5),
        'ffn2_w_up': nrm(ks[18], (DEPTH, D_MODEL, D_FF), D_MODEL ** -0.5),
        'ffn2_w_down': nrm(ks[19], (DEPTH, D_FF, D_MODEL), D_FF ** -0.5),
        'final_norm': gain(ks[20], (D_MODEL,)),
    }


def reference(x, meta_tokens, ffn1_norm, ffn1_w_gate, ffn1_w_up, ffn1_w_down,
              mix_norm, w_in, w_out, hgrn_lb_logits, hgrn_out_norm, attn_sinks,
              conv_dw_w, conv_dw_b, conv_ln_g, conv_ln_b,
              ffn2_norm, ffn2_w_gate, ffn2_w_up, ffn2_w_down, final_norm):
    Bsz = x.shape[0]
    meta = jnp.broadcast_to(meta_tokens[None].astype(x.dtype), (Bsz, N_META, D_MODEL))
    h = jnp.concatenate([meta, x], axis=1)
    lbs = jnp.cumsum(jax.nn.softmax(hgrn_lb_logits.astype(jnp.float32), axis=0), axis=0)
    lbs = lbs - lbs[0]
    split_points = np.cumsum(IN_SPLITS)[:-1].tolist()
    for l in range(DEPTH):
        h = h + FFN_HALF * swiglu(rms_norm(h, ffn1_norm[l]), ffn1_w_gate[l], ffn1_w_up[l], ffn1_w_down[l])
        p = rms_norm(h, mix_norm[l]) @ w_in[l]
        a_q, a_f, a_i, a_g, b_q, b_k, b_v, c_u = jnp.split(p, split_points, axis=-1)
        y_a = hgrn2_recurrence(a_q, a_f, a_i, a_g, lbs[l], hgrn_out_norm[l])
        y_b = sliding_window_sink_attention(b_q, b_k, b_v, attn_sinks[l])
        y_c = conformer_conv(c_u, conv_dw_w[l], conv_dw_b[l], conv_ln_g[l], conv_ln_b[l])
        h = h + jnp.concatenate([y_a, y_b, y_c], axis=-1) @ w_out[l]
        h = h + FFN_HALF * swiglu(rms_norm(h, ffn2_norm[l]), ffn2_w_gate[l], ffn2_w_up[l], ffn2_w_down[l])
    return rms_norm(h[:, N_META:], final_norm)
```

```python
import functools

import numpy as np
import jax
import jax.numpy as jnp
from jax import lax
from jax.experimental import pallas as pl
from jax.experimental.pallas import tpu as pltpu

D_MODEL = 1024
D_FF = 2816
N_META = 16
EPS = 1e-6
FFN_HALF = 0.5

A_HEADS = 6
A_DK = 64
A_WIDTH = 384
B_HEAD_DIM = 64
B_WIDTH = 384
B_Q_HEADS = 6
B_KV_HEADS = 2
B_KV_WIDTH = B_KV_HEADS * B_HEAD_DIM
B_WINDOW = 128
C_CHANNELS = 256
C_CONV_WIDTH = 31
IN_WIDTH = 2688

BLK = 128
PAD = BLK - N_META
LANE = 128
HALO = 32

PA_WIDTH = 3 * A_WIDTH
PB_WIDTH = B_WIDTH + 2 * B_KV_WIDTH
PC_WIDTH = 2 * C_CHANNELS

VMEM_LIMIT_BYTES = 56 * 1024 * 1024

_F32 = jnp.float32
_BF16 = jnp.bfloat16


def _dot(a, b):
    return jnp.dot(a, b, preferred_element_type=_F32)


def _dot_nt(a, b):
    return lax.dot_general(a, b, (((1,), (1,)), ((), ())), preferred_element_type=_F32)


def _rms(x, g):
    ms = jnp.mean(x * x, axis=-1, keepdims=True)
    return x * lax.rsqrt(ms + EPS) * g


def _silu(x):
    return x * (1.0 / (1.0 + jnp.exp(-x)))


def _sigmoid(x):
    return 1.0 / (1.0 + jnp.exp(-x))


def _ffn_kernel(*refs, pre, post, final, ff_chunk):
    it = iter(refs)
    h_ref = next(it)
    if pre:
        y_ref, wout_ref = next(it), next(it)
    g_ref, wg_ref, wu_ref, wd_ref = next(it), next(it), next(it), next(it)
    if post:
        g2_ref, win_ref = next(it), next(it)
    if final:
        gf_ref = next(it)
    hout_ref = next(it)
    if post:
        pf_ref, pa_ref, pb_ref, pc_ref = next(it), next(it), next(it), next(it)
    act_ref = next(it)

    h = h_ref[...]
    if pre:
        h = h + _dot(y_ref[...], wout_ref[...])
    xn = _rms(h, g_ref[...]).astype(_BF16)
    for c in range(D_FF // ff_chunk):
        cols = slice(c * ff_chunk, (c + 1) * ff_chunk)
        gate = _dot(xn, wg_ref[:, cols])
        up = _dot(xn, wu_ref[:, cols])
        act_ref[:, cols] = (_silu(gate) * up).astype(_BF16)
    h = h + FFN_HALF * _dot(act_ref[...], wd_ref[...])
    if final:
        hout_ref[...] = _rms(h, gf_ref[...])
    else:
        hout_ref[...] = h
    if post:
        xn2 = _rms(h, g2_ref[...]).astype(_BF16)
        pf_ref[...] = _dot(xn2, win_ref[:, 0:A_WIDTH])
        o = A_WIDTH
        pa_ref[...] = _dot(xn2, win_ref[:, o:o + PA_WIDTH]).astype(_BF16)
        o += PA_WIDTH
        pb_ref[...] = _dot(xn2, win_ref[:, o:o + PB_WIDTH]).astype(_BF16)
        o += PB_WIDTH
        pc_ref[...] = _dot(xn2, win_ref[:, o:o + PC_WIDTH]).astype(_BF16)


def _resident(shape):
    return pl.BlockSpec(shape, lambda i: (0,) * len(shape), pipeline_mode=pl.Buffered(1))


def _ffn_call(h, norm_g, wg, wu, wd, *, pre=None, post=None, final_g=None, tm=512, ff_chunk=704):
    rows = h.shape[0]
    assert rows % tm == 0 and D_FF % ff_chunk == 0
    row_spec = lambda w: pl.BlockSpec((tm, w), lambda i: (i, 0))
    args, specs = [h], [row_spec(D_MODEL)]
    if pre is not None:
        y, wout = pre
        args += [y, wout]
        specs += [row_spec(D_MODEL), _resident((D_MODEL, D_MODEL))]
    args += [norm_g, wg, wu, wd]
    specs += [_resident((1, D_MODEL)), _resident((D_MODEL, D_FF)), _resident((D_MODEL, D_FF)),
              _resident((D_FF, D_MODEL))]
    if post is not None:
        g2, win = post
        args += [g2, win]
        specs += [_resident((1, D_MODEL)), _resident((D_MODEL, IN_WIDTH))]
    if final_g is not None:
        args += [final_g]
        specs += [_resident((1, D_MODEL))]
    out_shape = [jax.ShapeDtypeStruct((rows, D_MODEL), _F32)]
    out_specs = [row_spec(D_MODEL)]
    if post is not None:
        for w, dt in ((A_WIDTH, _F32), (PA_WIDTH, _BF16), (PB_WIDTH, _BF16), (PC_WIDTH, _BF16)):
            out_shape.append(jax.ShapeDtypeStruct((rows, w), dt))
            out_specs.append(row_spec(w))
    kern = functools.partial(_ffn_kernel, pre=pre is not None, post=post is not None,
                             final=final_g is not None, ff_chunk=ff_chunk)
    return pl.pallas_call(
        kern,
        grid=(rows // tm,),
        in_specs=specs,
        out_specs=out_specs,
        out_shape=out_shape,
        scratch_shapes=[pltpu.VMEM((tm, D_FF), _BF16)],
        compiler_params=pltpu.CompilerParams(dimension_semantics=("parallel",),
                                             vmem_limit_bytes=VMEM_LIMIT_BYTES),
        name="ffn",
    )(*args)


def _split3(x):
    hi = x.astype(_BF16)
    r1 = x - hi.astype(_F32)
    mid = r1.astype(_BF16)
    lo = (r1 - mid.astype(_F32)).astype(_BF16)
    return hi, mid, lo


def _hgrn_block(n, pf_ref, pa_ref, lbc_ref, gain_ref, st_ref, g_ref):
    row = lax.broadcasted_iota(jnp.int32, (BLK, LANE), 0)
    col = lax.broadcasted_iota(jnp.int32, (BLK, LANE), 1)
    lane_lo = col < A_DK
    same_head = (row < A_DK) == (col < A_DK)
    t_xor_s = row ^ col

    z = pf_ref[0]
    log_lb, log_1m_lb, one_m_lb = lbc_ref[0:1, :], lbc_ref[1:2, :], lbc_ref[2:3, :]
    soft = jnp.log1p(jnp.exp(-jnp.abs(z)))
    log_sig = jnp.minimum(z, 0.0) - soft
    log_sig_neg = jnp.minimum(-z, 0.0) - soft
    b = log_1m_lb + log_sig
    log_f = jnp.maximum(log_lb, b) + jnp.log1p(jnp.exp(-jnp.abs(log_lb - b)))
    kk = one_m_lb * jnp.exp(log_sig_neg)

    tri = (row >= col).astype(_BF16)
    hi, mid, lo = _split3(log_f)
    g_cum = _dot(tri, hi) + _dot(tri, mid) + _dot(tri, lo)
    for p in range(A_HEADS // 2):
        g_ref[p] = g_cum[:, p * LANE:(p + 1) * LANE]
    g_tot = g_cum[BLK - 1:BLK, :]

    q = pa_ref[0, :, 0:A_WIDTH].astype(_F32)
    v = pa_ref[0, :, A_WIDTH:2 * A_WIDTH]
    gate = pa_ref[0, :, 2 * A_WIDTH:3 * A_WIDTH].astype(_F32)

    qg = (q * jnp.exp(g_cum)).astype(_BF16)
    kdec = (kk * jnp.exp(g_tot - g_cum)).astype(_BF16)
    decay_tot = jnp.exp(g_tot)
    kk_b = kk.astype(_BF16)

    outs = []
    for p in range(A_HEADS // 2):
        lanes = slice(p * LANE, (p + 1) * LANE)
        q_p, kk_p, g_p, v_p = q[:, lanes], kk[:, lanes], g_cum[:, lanes], v[:, lanes]
        q_lo = jnp.where(lane_lo, q_p, 0.0).astype(_BF16)
        q_hi = jnp.where(lane_lo, 0.0, q_p).astype(_BF16)
        diag = t_xor_s == 0
        acc = [jnp.where(diag, _dot_nt(q_lo, kk_b[:, lanes]), 0.0),
               jnp.where(diag, _dot_nt(q_hi, kk_b[:, lanes]), 0.0)]
        m = 1
        while m < BLK:
            groups = BLK // (2 * m)
            if groups == 1:
                g_at_ref = jnp.broadcast_to(g_ref[p, m - 1:m, :], (BLK, LANE))
            else:
                picked = g_ref[p, pl.ds(m - 1, groups, stride=2 * m), :]
                g_at_ref = jnp.broadcast_to(picked[:, None, :], (groups, 2 * m, LANE)).reshape(BLK, LANE)
            e = jnp.exp(-jnp.abs(g_p - g_at_ref))
            upper = (row & m) != 0
            q_l = jnp.where(upper, q_p * e, 0.0)
            k_l = jnp.where(upper, 0.0, kk_p * e).astype(_BF16)
            level = (t_xor_s & -m) == m
            acc[0] = jnp.where(level, _dot_nt(jnp.where(lane_lo, q_l, 0.0).astype(_BF16), k_l), acc[0])
            acc[1] = jnp.where(level, _dot_nt(jnp.where(lane_lo, 0.0, q_l).astype(_BF16), k_l), acc[1])
            m *= 2
        v_lo = jnp.where(lane_lo, v_p, jnp.zeros_like(v_p))
        v_hi = jnp.where(lane_lo, jnp.zeros_like(v_p), v_p)
        o_intra = _dot(acc[0].astype(_BF16), v_lo) + _dot(acc[1].astype(_BF16), v_hi)

        st = st_ref[p]
        o_inter = _dot_nt(qg[:, lanes], st.astype(_BF16))
        upd = _dot(v_p.astype(_F32).T.astype(_BF16), kdec[:, lanes])
        st_ref[p] = st * decay_tot[:, lanes] + jnp.where(same_head, upd, 0.0)
        outs.append(o_inter + o_intra)
    o = jnp.concatenate(outs, axis=1)

    r3 = lax.broadcasted_iota(jnp.int32, (A_WIDTH, A_WIDTH), 0) // A_DK
    c3 = lax.broadcasted_iota(jnp.int32, (A_WIDTH, A_WIDTH), 1) // A_DK
    avg = jnp.where(r3 == c3, 1.0 / A_DK, 0.0).astype(_BF16)
    sq_hi, sq_mid, sq_lo = _split3(o * o)
    ms = _dot(sq_hi, avg) + _dot(sq_mid, avg) + _dot(sq_lo, avg)
    return o * lax.rsqrt(ms + EPS) * gain_ref[...] * _silu(gate)


def _attn_block(n, pb_ref, pbp_ref, pbm_ref, sink_ref):
    kq = B_WIDTH
    kv_cols = slice(kq, kq + B_KV_WIDTH)
    vv_cols = slice(kq + B_KV_WIDTH, kq + 2 * B_KV_WIDTH)
    n_keys = 2 * BLK + N_META
    k_all = jnp.concatenate([pbp_ref[0, :, kv_cols], pb_ref[0, :, kv_cols],
                             pbm_ref[0, PAD:BLK, kv_cols]], axis=0)
    v_all = jnp.concatenate([pbp_ref[0, :, vv_cols], pb_ref[0, :, vv_cols],
                             pbm_ref[0, PAD:BLK, vv_cols]], axis=0)

    qi = lax.broadcasted_iota(jnp.int32, (BLK, n_keys), 0)
    kj = lax.broadcasted_iota(jnp.int32, (BLK, n_keys), 1)
    prev_ok = (kj < BLK) & (kj > qi) & (n >= 2)
    cur_ok = (kj >= BLK) & (kj < 2 * BLK) & (kj - BLK <= qi) & (n >= 1)
    meta_ok = (kj >= 2 * BLK) & ((n >= 1) | (kj - 2 * BLK + PAD <= qi))
    visible = prev_ok | cur_ok | meta_ok

    lane = lax.broadcasted_iota(jnp.int32, (BLK, LANE), 1)
    lane_lo = lane < B_HEAD_DIM
    lane_lo_k = lax.broadcasted_iota(jnp.int32, (n_keys, LANE), 1) < B_HEAD_DIM
    ones = jnp.ones((n_keys, LANE), _BF16)
    scale = B_HEAD_DIM ** -0.5
    outs = []
    for c in range(B_Q_HEADS // B_KV_HEADS):
        q_c = pb_ref[0, :, c * LANE:(c + 1) * LANE]
        halves = []
        for j in range(B_KV_HEADS):
            mine = lane_lo if j == 0 else ~lane_lo
            q_h = jnp.where(mine, q_c, jnp.zeros_like(q_c))
            s = _dot_nt(q_h, k_all) * scale
            s = jnp.where(visible, s, -jnp.inf)
            sink = sink_ref[c + (B_Q_HEADS // B_KV_HEADS) * j]
            mx = jnp.maximum(jnp.max(s, axis=-1, keepdims=True), sink)
            pr = jnp.exp(s - mx).astype(_BF16)
            num = _dot(pr, v_all)
            den = _dot(pr, ones) + jnp.exp(sink - mx)
            halves.append(num / den)
        outs.append(jnp.where(lane_lo, halves[0], halves[1]))
    return jnp.concatenate(outs, axis=1)


def _conv_block(n, pc_ref, pcp_ref, dww_ref, dwb_ref, lng_ref, lnb_ref):
    def glu(u):
        u = u.astype(_F32)
        return u[:, :C_CHANNELS] * _sigmoid(u[:, C_CHANNELS:])

    cur = glu(pc_ref[0])
    prev = glu(pcp_ref[0, BLK - HALO:BLK, :])
    prev = jnp.where(n >= 1, prev, 0.0)
    hist = jnp.concatenate([prev, cur], axis=0)
    first = HALO - (C_CONV_WIDTH - 1)
    acc = jnp.zeros((BLK, C_CHANNELS), _F32)
    for j in range(C_CONV_WIDTH):
        acc = acc + hist[first + j:first + j + BLK, :] * dww_ref[j:j + 1, :]
    hcv = acc + dwb_ref[...]
    mu = jnp.mean(hcv, axis=-1, keepdims=True)
    ctr = hcv - mu
    var = jnp.mean(ctr * ctr, axis=-1, keepdims=True)
    hn = ctr * lax.rsqrt(var + EPS) * lng_ref[...] + lnb_ref[...]
    return _silu(hn)


def _mixer_kernel(pf_ref, pa_ref, pb_ref, pbp_ref, pbm_ref, pc_ref, pcp_ref,
                  lbc_ref, gain_ref, sink_ref, dww_ref, dwb_ref, lng_ref, lnb_ref,
                  y_ref, st_ref, g_ref):
    n = pl.program_id(1)

    @pl.when(n == 0)
    def _():
        st_ref[...] = jnp.zeros_like(st_ref)

    y_a = _hgrn_block(n, pf_ref, pa_ref, lbc_ref, gain_ref, st_ref, g_ref)
    y_b = _attn_block(n, pb_ref, pbp_ref, pbm_ref, sink_ref)
    y_c = _conv_block(n, pc_ref, pcp_ref, dww_ref, dwb_ref, lng_ref, lnb_ref)
    y = jnp.concatenate([y_a, y_b, y_c], axis=1)
    row = lax.broadcasted_iota(jnp.int32, y.shape, 0)
    keep = (n >= 1) | (row >= PAD)
    y_ref[0] = jnp.where(keep, y, 0.0).astype(_BF16)


def _mixer_call(pf, pa, pb, pc, lbc, gain, sinks, dww, dwb, lng, lnb):
    bsz, lp, _ = pf.shape
    nb = lp // BLK
    cur = lambda w: pl.BlockSpec((1, BLK, w), lambda b, n: (b, n, 0))
    prev = lambda w: pl.BlockSpec((1, BLK, w), lambda b, n: (b, jnp.maximum(n - 1, 0), 0))
    first = lambda w: pl.BlockSpec((1, BLK, w), lambda b, n: (b, 0, 0))
    const = lambda r, w: pl.BlockSpec((r, w), lambda b, n: (0, 0))
    return pl.pallas_call(
        _mixer_kernel,
        grid=(bsz, nb),
        in_specs=[cur(A_WIDTH), cur(PA_WIDTH), cur(PB_WIDTH), prev(PB_WIDTH), first(PB_WIDTH),
                  cur(PC_WIDTH), prev(PC_WIDTH),
                  const(3, A_WIDTH), const(1, A_WIDTH),
                  pl.BlockSpec(memory_space=pltpu.SMEM),
                  const(C_CONV_WIDTH, C_CHANNELS), const(1, C_CHANNELS), const(1, C_CHANNELS),
                  const(1, C_CHANNELS)],
        out_specs=pl.BlockSpec((1, BLK, D_MODEL), lambda b, n: (b, n, 0)),
        out_shape=jax.ShapeDtypeStruct((bsz, lp, D_MODEL), _BF16),
        scratch_shapes=[pltpu.VMEM((A_HEADS // 2, LANE, LANE), _F32),
                        pltpu.VMEM((A_HEADS // 2, BLK, LANE), _F32)],
        compiler_params=pltpu.CompilerParams(dimension_semantics=("parallel", "arbitrary"),
                                             vmem_limit_bytes=VMEM_LIMIT_BYTES),
        name="mixer",
    )(pf, pa, pb, pb, pb, pc, pc, lbc, gain, sinks, dww, dwb, lng, lnb)


def _bq_perm():
    idx = []
    for c in range(B_Q_HEADS // B_KV_HEADS):
        for j in range(B_KV_HEADS):
            head = c + (B_Q_HEADS // B_KV_HEADS) * j
            idx.extend(range(head * B_HEAD_DIM, (head + 1) * B_HEAD_DIM))
    return np.asarray(idx)


def _win_columns():
    q0, f0, i0, g0 = 0, A_WIDTH, 2 * A_WIDTH, 3 * A_WIDTH
    b0 = 4 * A_WIDTH
    bk0 = b0 + B_WIDTH
    c0 = bk0 + 2 * B_KV_WIDTH
    r = lambda s, w: np.arange(s, s + w)
    return np.concatenate([r(f0, A_WIDTH), r(q0, A_WIDTH), r(i0, A_WIDTH), r(g0, A_WIDTH),
                           b0 + _bq_perm(), r(bk0, 2 * B_KV_WIDTH), r(c0, PC_WIDTH)])


def _wout_rows():
    return np.concatenate([np.arange(A_WIDTH), A_WIDTH + _bq_perm(),
                           np.arange(A_WIDTH + B_WIDTH, D_MODEL)])


def kernel(x, meta_tokens, ffn1_norm, ffn1_w_gate, ffn1_w_up, ffn1_w_down, mix_norm, w_in, w_out,
           hgrn_lb_logits, hgrn_out_norm, attn_sinks, conv_dw_w, conv_dw_b, conv_ln_g, conv_ln_b,
           ffn2_norm, ffn2_w_gate, ffn2_w_up, ffn2_w_down, final_norm):
    bsz, seq, _ = x.shape
    depth = w_in.shape[0]
    lp = PAD + N_META + seq
    assert seq % BLK == 0

    meta = jnp.broadcast_to(meta_tokens[None].astype(x.dtype), (bsz, N_META, D_MODEL))
    h = jnp.concatenate([jnp.zeros((bsz, PAD, D_MODEL), x.dtype), meta, x], axis=1)
    h = h.reshape(bsz * lp, D_MODEL)

    lbs = jnp.cumsum(jax.nn.softmax(hgrn_lb_logits.astype(_F32), axis=0), axis=0)
    lbs = lbs - lbs[0]
    lbc = jnp.stack([jnp.log(lbs), jnp.log1p(-lbs), 1.0 - lbs], axis=1)
    gain = jnp.tile(hgrn_out_norm.astype(_F32), (1, A_HEADS))[:, None, :]
    sinks = attn_sinks.astype(_F32)

    win_cols = _win_columns()
    wout_rows = _wout_rows()
    row2 = lambda a: a.astype(_F32)[None, :]

    tm = 512 if (bsz * lp) % 512 == 0 else BLK
    out = None
    for l in range(depth):
        w1 = (ffn1_w_gate[l].astype(_BF16), ffn1_w_up[l].astype(_BF16), ffn1_w_down[l].astype(_BF16))
        w2 = (ffn2_w_gate[l].astype(_BF16), ffn2_w_up[l].astype(_BF16), ffn2_w_down[l].astype(_BF16))
        win = w_in[l][:, win_cols].astype(_BF16)
        wout = w_out[l][wout_rows, :].astype(_BF16)

        h, pf, pa, pb, pc = _ffn_call(h, row2(ffn1_norm[l]), *w1, post=(row2(mix_norm[l]), win), tm=tm)
        shp = lambda a: a.reshape(bsz, lp, a.shape[-1])
        y = _mixer_call(shp(pf), shp(pa), shp(pb), shp(pc), lbc[l], gain[l], sinks[l],
                        conv_dw_w[l].astype(_F32), row2(conv_dw_b[l]), row2(conv_ln_g[l]),
                        row2(conv_ln_b[l]))
        y = y.reshape(bsz * lp, D_MODEL)
        last = l == depth - 1
        res = _ffn_call(h, row2(ffn2_norm[l]), *w2, pre=(y, wout),
                        final_g=row2(final_norm) if last else None, tm=tm)
        h = res[0]
    out = h.reshape(bsz, lp, D_MODEL)[:, PAD + N_META:]
    return out
```

```python
import functools

import numpy as np
import jax
import jax.numpy as jnp
from jax import lax
from jax.experimental import pallas as pl
from jax.experimental.pallas import tpu as pltpu

D_MODEL = 1024
D_FF = 2816
N_META = 16
EPS = 1e-6
LOG2_E = 1.4426950408889634
FFN_HALF = 0.5

A_HEADS = 6
A_DK = 64
A_WIDTH = 384
B_HEAD_DIM = 64
B_WIDTH = 384
B_Q_HEADS = 6
B_KV_HEADS = 2
B_KV_WIDTH = B_KV_HEADS * B_HEAD_DIM
B_WINDOW = 128
C_CHANNELS = 256
C_CONV_WIDTH = 31
IN_WIDTH = 2688

BLK = 128
PAD = BLK - N_META
LANE = 128
HALO = 32

PA_WIDTH = 3 * A_WIDTH
PB_WIDTH = B_WIDTH + 2 * B_KV_WIDTH
PC_WIDTH = 2 * C_CHANNELS

VMEM_LIMIT_BYTES = 56 * 1024 * 1024

_F32 = jnp.float32
_BF16 = jnp.bfloat16


def _dot(a, b):
    return jnp.dot(a, b, preferred_element_type=_F32)


def _dot_nt(a, b):
    return lax.dot_general(a, b, (((1,), (1,)), ((), ())), preferred_element_type=_F32)


def _rms(x, g):
    ms = jnp.mean(x * x, axis=-1, keepdims=True)
    return x * lax.rsqrt(ms + EPS) * g


def _silu(x):
    return x * (1.0 / (1.0 + jnp.exp(-x)))


def _sigmoid(x):
    return 1.0 / (1.0 + jnp.exp(-x))


def _ffn_kernel(*refs, pre, post, final, ff_chunk):
    it = iter(refs)
    h_ref = next(it)
    if pre:
        y_ref, wout_ref = next(it), next(it)
    g_ref, wg_ref, wu_ref, wd_ref = next(it), next(it), next(it), next(it)
    if post:
        g2_ref, win_ref = next(it), next(it)
    if final:
        gf_ref = next(it)
    hout_ref = next(it)
    if post:
        pf_ref, pa_ref, pb_ref, pc_ref = next(it), next(it), next(it), next(it)
    act_ref = next(it)

    h = h_ref[...]
    if pre:
        h = h + _dot(y_ref[...], wout_ref[...])
    xn = _rms(h, g_ref[...]).astype(_BF16)
    for c in range(D_FF // ff_chunk):
        cols = slice(c * ff_chunk, (c + 1) * ff_chunk)
        gate = _dot(xn, wg_ref[:, cols])
        up = _dot(xn, wu_ref[:, cols])
        act_ref[:, cols] = (_silu(gate) * up).astype(_BF16)
    h = h + FFN_HALF * _dot(act_ref[...], wd_ref[...])
    if final:
        hout_ref[...] = _rms(h, gf_ref[...])
    else:
        hout_ref[...] = h
    if post:
        xn2 = _rms(h, g2_ref[...]).astype(_BF16)
        pf_ref[...] = _dot(xn2, win_ref[:, 0:A_WIDTH])
        o = A_WIDTH
        pa_ref[...] = _dot(xn2, win_ref[:, o:o + PA_WIDTH]).astype(_BF16)
        o += PA_WIDTH
        pb_ref[...] = _dot(xn2, win_ref[:, o:o + PB_WIDTH]).astype(_BF16)
        o += PB_WIDTH
        pc_ref[...] = _dot(xn2, win_ref[:, o:o + PC_WIDTH]).astype(_BF16)


def _resident(shape):
    return pl.BlockSpec(shape, lambda i: (0,) * len(shape), pipeline_mode=pl.Buffered(1))


def _ffn_call(h, norm_g, wg, wu, wd, *, pre=None, post=None, final_g=None, tm=512, ff_chunk=704):
    rows = h.shape[0]
    assert rows % tm == 0 and D_FF % ff_chunk == 0
    row_spec = lambda w: pl.BlockSpec((tm, w), lambda i: (i, 0))
    args, specs = [h], [row_spec(D_MODEL)]
    if pre is not None:
        y, wout = pre
        args += [y, wout]
        specs += [row_spec(D_MODEL), _resident((D_MODEL, D_MODEL))]
    args += [norm_g, wg, wu, wd]
    specs += [_resident((1, D_MODEL)), _resident((D_MODEL, D_FF)), _resident((D_MODEL, D_FF)),
              _resident((D_FF, D_MODEL))]
    if post is not None:
        g2, win = post
        args += [g2, win]
        specs += [_resident((1, D_MODEL)), _resident((D_MODEL, IN_WIDTH))]
    if final_g is not None:
        args += [final_g]
        specs += [_resident((1, D_MODEL))]
    out_shape = [jax.ShapeDtypeStruct((rows, D_MODEL), _F32)]
    out_specs = [row_spec(D_MODEL)]
    if post is not None:
        for w, dt in ((A_WIDTH, _F32), (PA_WIDTH, _BF16), (PB_WIDTH, _BF16), (PC_WIDTH, _BF16)):
            out_shape.append(jax.ShapeDtypeStruct((rows, w), dt))
            out_specs.append(row_spec(w))
    kern = functools.partial(_ffn_kernel, pre=pre is not None, post=post is not None,
                             final=final_g is not None, ff_chunk=ff_chunk)
    return pl.pallas_call(
        kern,
        grid=(rows // tm,),
        in_specs=specs,
        out_specs=out_specs,
        out_shape=out_shape,
        scratch_shapes=[pltpu.VMEM((tm, D_FF), _BF16)],
        compiler_params=pltpu.CompilerParams(dimension_semantics=("parallel",),
                                             vmem_limit_bytes=VMEM_LIMIT_BYTES),
        name="ffn",
    )(*args)


N_LEVELS = 7
N_GROUPS = N_LEVELS + 2
DECAY_TERMS = 2
CONV_SUB = 8
HIST_ROWS = HALO + BLK


def _split_bf16(x, terms):
    out = []
    for _ in range(terms - 1):
        part = x.astype(_BF16)
        out.append(part)
        x = x - part.astype(_F32)
    out.append(x.astype(_BF16))
    return out


def _decay_matrix():
    c = np.zeros((N_GROUPS, BLK, BLK), np.float32)
    for t in range(BLK):
        c[0, t, :t + 1] = 1.0
        for l in range(1, N_LEVELS + 1):
            m = 1 << (l - 1)
            ref = (t // (2 * m)) * 2 * m + m - 1
            if t > ref:
                c[l, t, ref + 1:t + 1] = 1.0
            else:
                c[l, t, t + 1:ref + 1] = 1.0
        c[N_GROUPS - 1, t, t + 1:] = 1.0
    c = c.reshape(N_GROUPS * BLK, BLK)
    return np.concatenate([c] * DECAY_TERMS, axis=1)


def _level_matrix():
    t = np.arange(BLK)[:, None]
    s = np.arange(BLK)[None, :]
    x = t ^ s
    lvl = np.where(x == 0, 0, np.floor(np.log2(np.maximum(x, 1))) + 1)
    lvl = np.where(s > t, -1, lvl).astype(np.float32)
    return np.concatenate([lvl, lvl], axis=1)


def _head_mean_matrix():
    h = np.arange(A_WIDTH) // A_DK
    m = np.where(h[:, None] == h[None, :], 1.0 / A_DK, 0.0).astype(np.float32)
    return np.concatenate([m, m], axis=0)


def _attn_bias():
    n_keys = 2 * BLK + N_META
    qi = np.arange(BLK)[:, None]
    kj = np.arange(n_keys)[None, :]
    out = []
    for n in range(3):
        prev_ok = (kj < BLK) & (kj > qi) & (n >= 2)
        cur_ok = (kj >= BLK) & (kj < 2 * BLK) & (kj - BLK <= qi) & (n >= 1)
        meta_ok = (kj >= 2 * BLK) & ((n >= 1) | (kj - 2 * BLK + PAD <= qi))
        out.append(np.where(prev_ok | cur_ok | meta_ok, 0.0, -1e30).T)
    return np.stack(out).astype(np.float32)


def _hgrn_block(pf_ref, pa_ref, lbc_ref, gain_ref, cdec_ref, lvl_ref, avg_ref, st_ref, r_ref):
    row = lax.broadcasted_iota(jnp.int32, (BLK, LANE), 0)
    col = lax.broadcasted_iota(jnp.int32, (BLK, LANE), 1)
    same_head = (row < A_DK) == (col < A_DK)
    lane1 = lax.broadcasted_iota(jnp.int32, (1, LANE), 1)
    lo_b = (lane1 < A_DK).astype(_BF16)
    hi_b = (lane1 >= A_DK).astype(_BF16)

    z = pf_ref[0]
    log_lb, log_1m_lb, one_m_lb = lbc_ref[0:1, :], lbc_ref[1:2, :], lbc_ref[2:3, :]
    soft = jnp.log(1.0 + jnp.exp(-jnp.abs(z)))
    log_sig = jnp.minimum(z, 0.0) - soft
    log_sig_neg = jnp.minimum(-z, 0.0) - soft
    b = log_1m_lb + log_sig
    log_f = jnp.maximum(log_lb, b) + jnp.log(1.0 + jnp.exp(-jnp.abs(log_lb - b)))
    kk_b = (one_m_lb * jnp.exp(log_sig_neg)).astype(_BF16)
    yield

    r_ref[...] = _dot(cdec_ref[...], jnp.concatenate(_split_bf16(log_f * LOG2_E, DECAY_TERMS), axis=0))
    grp = lambda g, lanes: r_ref[g * BLK:(g + 1) * BLK, lanes]
    decay_tot = jnp.exp2(r_ref[BLK - 1:BLK, :])
    lvl = lvl_ref[...]
    yield

    outs = []
    for p in range(A_HEADS // 2):
        lanes = slice(p * LANE, (p + 1) * LANE)
        q_b = pa_ref[0, :, lanes]
        v_b = pa_ref[0, :, A_WIDTH + p * LANE:A_WIDTH + (p + 1) * LANE]
        k_lo, k_hi = kk_b[:, lanes] * lo_b, kk_b[:, lanes] * hi_b
        a_cat = jnp.where(lvl == 0.0, _dot_nt(q_b, jnp.concatenate([k_lo, k_hi], axis=0)).astype(_BF16),
                          jnp.zeros((BLK, 2 * BLK), _BF16))
        yield
        for l in range(1, N_LEVELS + 1):
            e = jnp.exp2(grp(l, lanes)).astype(_BF16)
            k_cat = jnp.concatenate([k_lo * e, k_hi * e], axis=0)
            a_cat = jnp.where(lvl == float(l), _dot_nt(q_b * e, k_cat).astype(_BF16), a_cat)
            yield
        v_cat = jnp.concatenate([v_b * lo_b, v_b * hi_b], axis=0)
        o_intra = _dot(a_cat, v_cat)
        yield

        st = st_ref[p]
        qg = q_b * jnp.exp2(grp(0, lanes)).astype(_BF16)
        o_inter = _dot_nt(qg, st.astype(_BF16))
        kdec = kk_b[:, lanes] * jnp.exp2(grp(N_GROUPS - 1, lanes)).astype(_BF16)
        upd = lax.dot_general(v_b, kdec, (((0,), (0,)), ((), ())), preferred_element_type=_F32)
        st_ref[p] = st * decay_tot[:, lanes] + jnp.where(same_head, upd, 0.0)
        outs.append(o_inter + o_intra)
        yield
    o = jnp.concatenate(outs, axis=1)

    ms = _dot(jnp.concatenate(_split_bf16(o * o, 2), axis=1), avg_ref[...])
    gate = pa_ref[0, :, 2 * A_WIDTH:3 * A_WIDTH].astype(_F32)
    return o * lax.rsqrt(ms + EPS) * gain_ref[...] * _silu(gate)


def _attn_block(n, pb_ref, bias_ref, sink_ref, kv_ref, meta_ref):
    kv_cur = pb_ref[0, :, B_WIDTH:B_WIDTH + 2 * B_KV_WIDTH]

    kv_all = jnp.concatenate([kv_ref[(n + 1) % 2], kv_cur, meta_ref[...]], axis=0)
    kv_ref[n % 2] = kv_cur
    k_all, v_all = kv_all[:, :B_KV_WIDTH], kv_all[:, B_KV_WIDTH:]
    n_keys = 2 * BLK + N_META

    lane1 = lax.broadcasted_iota(jnp.int32, (1, LANE), 1)
    lo_b = (lane1 < B_HEAD_DIM).astype(_BF16)
    hi_b = (lane1 >= B_HEAD_DIM).astype(_BF16)
    n_col = B_Q_HEADS // B_KV_HEADS
    q_cat = jnp.concatenate([pb_ref[0, :, c * LANE:(c + 1) * LANE] * (lo_b if j == 0 else hi_b)
                             for j in range(B_KV_HEADS) for c in range(n_col)], axis=0)
    sink = jnp.concatenate([jnp.full((1, BLK), sink_ref[h], _F32) for h in range(B_Q_HEADS)], axis=1)
    yield
    s = _dot_nt(k_all, q_cat) + jnp.concatenate([bias_ref[0]] * B_Q_HEADS, axis=1)
    yield
    mx = jnp.maximum(jnp.max(s, axis=0, keepdims=True), sink)
    yield
    pr = jnp.exp(s - mx).astype(_BF16)
    yield
    num = lax.dot_general(v_all, pr, (((0,), (0,)), ((), ())), preferred_element_type=_F32)
    den = _dot(jnp.ones((16, n_keys), _BF16), pr)[0:1, :] + jnp.exp(sink - mx)
    yield
    o = num / den
    yield
    half = B_HEAD_DIM
    outs = [jnp.concatenate([o[:half, c * BLK:(c + 1) * BLK],
                             o[half:, (n_col + c) * BLK:(n_col + c + 1) * BLK]], axis=0).T
            for c in range(n_col)]
    return jnp.concatenate(outs, axis=1)


def _conv_block(pc_ref, dww_ref, dwb_ref, lng_ref, lnb_ref, hist_ref):
    u = pc_ref[0].astype(_F32)
    hist_ref[HALO:HIST_ROWS, :] = u[:, :C_CHANNELS] * _sigmoid(u[:, C_CHANNELS:])
    first = HALO - (C_CONV_WIDTH - 1)
    acc = jnp.zeros((BLK, C_CHANNELS), _F32)
    hist = hist_ref[...]
    hist_ref[0:HALO, :] = hist[BLK:HIST_ROWS, :]
    yield
    for b in range(CONV_SUB):
        phase = pltpu.roll(hist, HIST_ROWS - (first + b), axis=0)
        for a in range((C_CONV_WIDTH - 1 - b) // CONV_SUB + 1):
            j = CONV_SUB * a + b
            acc = acc + phase[8 * a:8 * a + BLK, :] * dww_ref[j:j + 1, :]
            yield
    hcv = acc + dwb_ref[...]
    mu = jnp.mean(hcv, axis=-1, keepdims=True)
    ctr = hcv - mu
    yield
    var = jnp.mean(ctr * ctr, axis=-1, keepdims=True)
    hn = ctr * lax.rsqrt(var + EPS) * lng_ref[...] + lnb_ref[...]
    return _silu(hn)


def _interleave(*gens):
    results = [None] * len(gens)
    active = list(enumerate(gens))
    while active:
        still = []
        for i, g in active:
            try:
                next(g)
                still.append((i, g))
            except StopIteration as stop:
                results[i] = stop.value
        active = still
    return results


def _mixer_kernel(pf_ref, pa_ref, pb_ref, pc_ref, lbc_ref, gain_ref, sink_ref, dww_ref, dwb_ref,
                  lng_ref, lnb_ref, cdec_ref, lvl_ref, avg_ref, bias_ref,
                  y_ref, st_ref, r_ref, kv_ref, meta_ref, hist_ref):
    n = pl.program_id(1)

    @pl.when(n == 0)
    def _():
        st_ref[...] = jnp.zeros_like(st_ref)
        hist_ref[0:HALO, :] = jnp.zeros((HALO, C_CHANNELS), _F32)
        kv_first = pb_ref[0, :, B_WIDTH:B_WIDTH + 2 * B_KV_WIDTH]
        meta_ref[...] = kv_first[PAD:BLK, :]
        kv_ref[1] = kv_first

    y_a, y_b, y_c = _interleave(
        _hgrn_block(pf_ref, pa_ref, lbc_ref, gain_ref, cdec_ref, lvl_ref, avg_ref, st_ref, r_ref),
        _attn_block(n, pb_ref, bias_ref, sink_ref, kv_ref, meta_ref),
        _conv_block(pc_ref, dww_ref, dwb_ref, lng_ref, lnb_ref, hist_ref))
    y_ref[0, :, 0:A_WIDTH] = y_a.astype(_BF16)
    y_ref[0, :, A_WIDTH:A_WIDTH + B_WIDTH] = y_b.astype(_BF16)
    y_ref[0, :, A_WIDTH + B_WIDTH:D_MODEL] = y_c.astype(_BF16)

    @pl.when(n == 0)
    def _():
        y_ref[0, 0:PAD, :] = jnp.zeros((PAD, D_MODEL), _BF16)


def _mixer_call(pf, pa, pb, pc, lbc, gain, sinks, dww, dwb, lng, lnb):
    bsz, lp, _ = pf.shape
    nb = lp // BLK
    cur = lambda w: pl.BlockSpec((1, BLK, w), lambda b, n: (b, n, 0))
    const = lambda r, w: pl.BlockSpec((r, w), lambda b, n: (0, 0))
    cdec = jnp.asarray(_decay_matrix(), _BF16)
    lvl = jnp.asarray(_level_matrix(), _BF16)
    avg = jnp.asarray(_head_mean_matrix(), _BF16)
    bias = jnp.asarray(_attn_bias())
    n_keys = 2 * BLK + N_META
    return pl.pallas_call(
        _mixer_kernel,
        grid=(bsz, nb),
        in_specs=[cur(A_WIDTH), cur(PA_WIDTH), cur(PB_WIDTH), cur(PC_WIDTH),
                  const(3, A_WIDTH), const(1, A_WIDTH),
                  pl.BlockSpec(memory_space=pltpu.SMEM),
                  const(C_CONV_WIDTH, C_CHANNELS), const(1, C_CHANNELS), const(1, C_CHANNELS),
                  const(1, C_CHANNELS),
                  const(*cdec.shape), const(*lvl.shape), const(*avg.shape),
                  pl.BlockSpec((1, n_keys, BLK), lambda b, n: (jnp.minimum(n, 2), 0, 0))],
        out_specs=pl.BlockSpec((1, BLK, D_MODEL), lambda b, n: (b, n, 0)),
        out_shape=jax.ShapeDtypeStruct((bsz, lp, D_MODEL), _BF16),
        scratch_shapes=[pltpu.VMEM((A_HEADS // 2, LANE, LANE), _F32),
                        pltpu.VMEM((N_GROUPS * BLK, A_WIDTH), _F32),
                        pltpu.VMEM((2, BLK, 2 * B_KV_WIDTH), _BF16),
                        pltpu.VMEM((N_META, 2 * B_KV_WIDTH), _BF16),
                        pltpu.VMEM((HIST_ROWS, C_CHANNELS), _F32)],
        compiler_params=pltpu.CompilerParams(dimension_semantics=("parallel", "arbitrary"),
                                             vmem_limit_bytes=VMEM_LIMIT_BYTES),
        name="mixer",
    )(pf, pa, pb, pc, lbc, gain, sinks, dww, dwb, lng, lnb, cdec, lvl, avg, bias)


def _bq_perm():
    idx = []
    for c in range(B_Q_HEADS // B_KV_HEADS):
        for j in range(B_KV_HEADS):
            head = c + (B_Q_HEADS // B_KV_HEADS) * j
            idx.extend(range(head * B_HEAD_DIM, (head + 1) * B_HEAD_DIM))
    return np.asarray(idx)


def _win_columns():
    q0, f0, i0, g0 = 0, A_WIDTH, 2 * A_WIDTH, 3 * A_WIDTH
    b0 = 4 * A_WIDTH
    bk0 = b0 + B_WIDTH
    c0 = bk0 + 2 * B_KV_WIDTH
    r = lambda s, w: np.arange(s, s + w)
    return np.concatenate([r(f0, A_WIDTH), r(q0, A_WIDTH), r(i0, A_WIDTH), r(g0, A_WIDTH),
                           b0 + _bq_perm(), r(bk0, 2 * B_KV_WIDTH), r(c0, PC_WIDTH)])


def _wout_rows():
    return np.concatenate([np.arange(A_WIDTH), A_WIDTH + _bq_perm(),
                           np.arange(A_WIDTH + B_WIDTH, D_MODEL)])


def kernel(x, meta_tokens, ffn1_norm, ffn1_w_gate, ffn1_w_up, ffn1_w_down, mix_norm, w_in, w_out,
           hgrn_lb_logits, hgrn_out_norm, attn_sinks, conv_dw_w, conv_dw_b, conv_ln_g, conv_ln_b,
           ffn2_norm, ffn2_w_gate, ffn2_w_up, ffn2_w_down, final_norm):
    bsz, seq, _ = x.shape
    depth = w_in.shape[0]
    lp = PAD + N_META + seq
    assert seq % BLK == 0

    meta = jnp.broadcast_to(meta_tokens[None].astype(x.dtype), (bsz, N_META, D_MODEL))
    h = jnp.concatenate([jnp.zeros((bsz, PAD, D_MODEL), x.dtype), meta, x], axis=1)
    h = h.reshape(bsz * lp, D_MODEL)

    lbs = jnp.cumsum(jax.nn.softmax(hgrn_lb_logits.astype(_F32), axis=0), axis=0)
    lbs = lbs - lbs[0]
    lbc = jnp.stack([jnp.log(lbs), jnp.log1p(-lbs), 1.0 - lbs], axis=1)
    gain = jnp.tile(hgrn_out_norm.astype(_F32), (1, A_HEADS))[:, None, :]
    sinks = attn_sinks.astype(_F32)

    win_cols = _win_columns()
    wout_rows = _wout_rows()
    row2 = lambda a: a.astype(_F32)[None, :]
    qscale = np.ones((IN_WIDTH,), np.float32)
    qscale[A_WIDTH + PA_WIDTH:A_WIDTH + PA_WIDTH + B_WIDTH] = B_HEAD_DIM ** -0.5

    tm = 512 if (bsz * lp) % 512 == 0 else BLK
    for l in range(depth):
        w1 = (ffn1_w_gate[l].astype(_BF16), ffn1_w_up[l].astype(_BF16), ffn1_w_down[l].astype(_BF16))
        w2 = (ffn2_w_gate[l].astype(_BF16), ffn2_w_up[l].astype(_BF16), ffn2_w_down[l].astype(_BF16))
        win = (w_in[l][:, win_cols] * qscale).astype(_BF16)
        wout = w_out[l][wout_rows, :].astype(_BF16)

        h, pf, pa, pb, pc = _ffn_call(h, row2(ffn1_norm[l]), *w1, post=(row2(mix_norm[l]), win), tm=tm)
        shp = lambda a: a.reshape(bsz, lp, a.shape[-1])
        y = _mixer_call(shp(pf), shp(pa), shp(pb), shp(pc), lbc[l], gain[l], sinks[l],
                        conv_dw_w[l].astype(_F32), row2(conv_dw_b[l]), row2(conv_ln_g[l]),
                        row2(conv_ln_b[l]))
        y = y.reshape(bsz * lp, D_MODEL)
        last = l == depth - 1
        res = _ffn_call(h, row2(ffn2_norm[l]), *w2, pre=(y, wout),
                        final_g=row2(final_norm) if last else None, tm=tm)
        h = res[0]
    return h.reshape(bsz, lp, D_MODEL)[:, PAD + N_META:]
```

```python
import functools

import numpy as np
import jax
import jax.numpy as jnp
from jax import lax
from jax.experimental import pallas as pl
from jax.experimental.pallas import tpu as pltpu

D_MODEL = 1024
D_FF = 2816
N_META = 16
EPS = 1e-6
LOG2_E = 1.4426950408889634
FFN_HALF = 0.5

A_HEADS = 6
A_DK = 64
A_WIDTH = 384
B_HEAD_DIM = 64
B_WIDTH = 384
B_Q_HEADS = 6
B_KV_HEADS = 2
B_KV_WIDTH = B_KV_HEADS * B_HEAD_DIM
B_WINDOW = 128
C_CHANNELS = 256
C_CONV_WIDTH = 31
IN_WIDTH = 2688

BLK = 128
PAD = BLK - N_META
LANE = 128
HALO = 32

PA_WIDTH = 3 * A_WIDTH
PB_WIDTH = B_WIDTH + 2 * B_KV_WIDTH
PC_WIDTH = 2 * C_CHANNELS

VMEM_LIMIT_BYTES = 56 * 1024 * 1024

_F32 = jnp.float32
_BF16 = jnp.bfloat16


def _dot(a, b):
    return jnp.dot(a, b, preferred_element_type=_F32)


def _dot_nt(a, b):
    return lax.dot_general(a, b, (((1,), (1,)), ((), ())), preferred_element_type=_F32)


def _rms(x, g):
    ms = jnp.mean(x * x, axis=-1, keepdims=True)
    return x * lax.rsqrt(ms + EPS) * g


def _silu(x):
    return x * (1.0 / (1.0 + jnp.exp(-x)))


def _sigmoid(x):
    return 1.0 / (1.0 + jnp.exp(-x))


def _ffn_kernel(*refs, pre, post, final, ff_chunk):
    it = iter(refs)
    h_ref = next(it)
    if pre:
        y_ref, wout_ref = next(it), next(it)
    g_ref, wg_ref, wu_ref, wd_ref = next(it), next(it), next(it), next(it)
    if post:
        g2_ref, win_ref = next(it), next(it)
    if final:
        gf_ref = next(it)
    hout_ref = next(it)
    if post:
        pf_ref, pa_ref, pb_ref, pc_ref = next(it), next(it), next(it), next(it)
    act_ref = next(it)

    h = h_ref[...]
    if pre:
        h = h + _dot(y_ref[...], wout_ref[...])
    xn = _rms(h, g_ref[...]).astype(_BF16)
    for start in range(0, D_FF, ff_chunk):
        cols = slice(start, min(start + ff_chunk, D_FF))
        gate = _dot(xn, wg_ref[:, cols])
        up = _dot(xn, wu_ref[:, cols])
        act_ref[:, cols] = (_silu(gate) * up).astype(_BF16)
    h = h + FFN_HALF * _dot(act_ref[...], wd_ref[...])
    if final:
        hout_ref[...] = _rms(h, gf_ref[...])
    else:
        hout_ref[...] = h
    if post:
        xn2 = _rms(h, g2_ref[...]).astype(_BF16)
        pf_ref[...] = _dot(xn2, win_ref[:, 0:A_WIDTH])
        o = A_WIDTH
        pa_ref[...] = _dot(xn2, win_ref[:, o:o + PA_WIDTH]).astype(_BF16)
        o += PA_WIDTH
        pb_ref[...] = _dot(xn2, win_ref[:, o:o + PB_WIDTH]).astype(_BF16)
        o += PB_WIDTH
        pc_ref[...] = _dot(xn2, win_ref[:, o:o + PC_WIDTH]).astype(_BF16)


def _resident(shape):
    return pl.BlockSpec(shape, lambda i: (0,) * len(shape), pipeline_mode=pl.Buffered(1))


def _ffn_call(h, norm_g, wg, wu, wd, *, pre=None, post=None, final_g=None, tm=512, ff_chunk=768):
    rows = h.shape[0]
    assert rows % tm == 0
    row_spec = lambda w: pl.BlockSpec((tm, w), lambda i: (i, 0))
    args, specs = [h], [row_spec(D_MODEL)]
    if pre is not None:
        y, wout = pre
        args += [y, wout]
        specs += [row_spec(D_MODEL), _resident((D_MODEL, D_MODEL))]
    args += [norm_g, wg, wu, wd]
    specs += [_resident((1, D_MODEL)), _resident((D_MODEL, D_FF)), _resident((D_MODEL, D_FF)),
              _resident((D_FF, D_MODEL))]
    if post is not None:
        g2, win = post
        args += [g2, win]
        specs += [_resident((1, D_MODEL)), _resident((D_MODEL, IN_WIDTH))]
    if final_g is not None:
        args += [final_g]
        specs += [_resident((1, D_MODEL))]
    out_shape = [jax.ShapeDtypeStruct((rows, D_MODEL), _F32)]
    out_specs = [row_spec(D_MODEL)]
    if post is not None:
        for w, dt in ((A_WIDTH, _F32), (PA_WIDTH, _BF16), (PB_WIDTH, _BF16), (PC_WIDTH, _BF16)):
            out_shape.append(jax.ShapeDtypeStruct((rows, w), dt))
            out_specs.append(row_spec(w))
    kern = functools.partial(_ffn_kernel, pre=pre is not None, post=post is not None,
                             final=final_g is not None, ff_chunk=ff_chunk)
    return pl.pallas_call(
        kern,
        grid=(rows // tm,),
        in_specs=specs,
        out_specs=out_specs,
        out_shape=out_shape,
        scratch_shapes=[pltpu.VMEM((tm, D_FF), _BF16)],
        compiler_params=pltpu.CompilerParams(dimension_semantics=("parallel",),
                                             vmem_limit_bytes=VMEM_LIMIT_BYTES),
        name="ffn",
    )(*args)


N_LEVELS = 7
N_GROUPS = N_LEVELS + 2
DECAY_TERMS = 2
HIST_ROWS = HALO + BLK
TILE_BLOCKS = 3


def _split_bf16(x, terms):
    out = []
    for _ in range(terms - 1):
        part = x.astype(_BF16)
        out.append(part)
        x = x - part.astype(_F32)
    out.append(x.astype(_BF16))
    return out


def _decay_matrix():
    c = np.zeros((N_GROUPS, BLK, BLK), np.float32)
    for t in range(BLK):
        c[0, t, :t + 1] = 1.0
        for l in range(1, N_LEVELS + 1):
            m = 1 << (l - 1)
            ref = (t // (2 * m)) * 2 * m + m - 1
            if t > ref:
                c[l, t, ref + 1:t + 1] = 1.0
            else:
                c[l, t, t + 1:ref + 1] = 1.0
        c[N_GROUPS - 1, t, t + 1:] = 1.0
    c = c.reshape(N_GROUPS * BLK, BLK)
    return np.concatenate([c] * DECAY_TERMS, axis=1)


def _level_matrix():
    t = np.arange(BLK)[:, None]
    s = np.arange(BLK)[None, :]
    x = t ^ s
    lvl = np.where(x == 0, 0, np.floor(np.log2(np.maximum(x, 1))) + 1)
    lvl = np.where(s > t, -1, lvl).astype(np.float32)
    return np.concatenate([lvl, lvl], axis=1)


def _head_mean_matrix():
    h = np.arange(A_WIDTH) // A_DK
    m = np.where(h[:, None] == h[None, :], 1.0 / A_DK, 0.0).astype(np.float32)
    return np.concatenate([m, m], axis=0)


def _conv_tap_matrix():
    first = HALO - (C_CONV_WIDTH - 1)
    m = np.zeros((BLK, C_CONV_WIDTH, HIST_ROWS), np.float32)
    for t in range(BLK):
        for j in range(C_CONV_WIDTH):
            m[t, j, t + first + j] = 1.0
    return m.reshape(BLK, C_CONV_WIDTH * HIST_ROWS)


def _attn_bias():
    n_keys = 2 * BLK + N_META
    qi = np.arange(BLK)[:, None]
    kj = np.arange(n_keys)[None, :]
    out = []
    for n in range(3):
        prev_ok = (kj < BLK) & (kj > qi) & (n >= 2)
        cur_ok = (kj >= BLK) & (kj < 2 * BLK) & (kj - BLK <= qi) & (n >= 1)
        meta_ok = (kj >= 2 * BLK) & ((n >= 1) | (kj - 2 * BLK + PAD <= qi))
        out.append(np.where(prev_ok | cur_ok | meta_ok, 0.0, -1e30).T)
    return np.stack(out).astype(np.float32)


def _hgrn_block(rows, pf_ref, pa_ref, lbc_ref, gain_ref, cdec_ref, lvl_ref, avg_ref, st_ref, r_ref):
    row = lax.broadcasted_iota(jnp.int32, (BLK, LANE), 0)
    col = lax.broadcasted_iota(jnp.int32, (BLK, LANE), 1)
    same_head = (row < A_DK) == (col < A_DK)
    lane1 = lax.broadcasted_iota(jnp.int32, (1, LANE), 1)
    lo_f = (lane1 < A_DK).astype(_F32)
    hi_f = 1.0 - lo_f
    lo_b, hi_b = lo_f.astype(_BF16), hi_f.astype(_BF16)

    z = pf_ref[0, rows, :]
    log_lb, log_1m_lb, one_m_lb = lbc_ref[0:1, :], lbc_ref[1:2, :], lbc_ref[2:3, :]
    soft = jnp.log(1.0 + jnp.exp(-jnp.abs(z)))
    log_sig = jnp.minimum(z, 0.0) - soft
    log_sig_neg = jnp.minimum(-z, 0.0) - soft
    b = log_1m_lb + log_sig
    log_f = jnp.maximum(log_lb, b) + jnp.log(1.0 + jnp.exp(-jnp.abs(log_lb - b)))
    kk = one_m_lb * jnp.exp(log_sig_neg)
    kk_b = kk.astype(_BF16)
    yield

    log_f_terms = jnp.concatenate(_split_bf16(log_f * LOG2_E, DECAY_TERMS), axis=0)
    half_rows = N_GROUPS * BLK // 2
    r_ref[0:half_rows, :] = _dot(cdec_ref[0:half_rows, :], log_f_terms)
    r_ref[half_rows:, :] = _dot(cdec_ref[half_rows:, :], log_f_terms)
    grp = lambda g, lanes: r_ref[g * BLK:(g + 1) * BLK, lanes]
    decay_tot = jnp.exp2(r_ref[BLK - 1:BLK, :])
    lvl = lvl_ref[...]
    yield

    outs = []
    for p in range(A_HEADS // 2):
        lanes = slice(p * LANE, (p + 1) * LANE)
        q_b = pa_ref[0, rows, lanes]
        v_b = pa_ref[0, rows, A_WIDTH + p * LANE:A_WIDTH + (p + 1) * LANE]
        k_both = jnp.concatenate([(kk[:, lanes] * lo_f).astype(_BF16),
                                  (kk[:, lanes] * hi_f).astype(_BF16)], axis=0)
        a_cat = jnp.where(lvl == 0.0, _dot_nt(q_b, k_both).astype(_BF16),
                          jnp.zeros((BLK, 2 * BLK), _BF16))
        yield
        for l in range(1, N_LEVELS + 1):
            e = jnp.exp2(grp(l, lanes)).astype(_BF16)
            k_cat = k_both * jnp.concatenate([e, e], axis=0)
            a_cat = jnp.where(lvl == float(l), _dot_nt(q_b * e, k_cat).astype(_BF16), a_cat)
            yield
        v_cat = jnp.concatenate([v_b * lo_b, v_b * hi_b], axis=0)
        o_intra = _dot(a_cat, v_cat)
        yield

        st = st_ref[p]
        qg = q_b * jnp.exp2(grp(0, lanes)).astype(_BF16)
        o_inter = _dot_nt(qg, st.astype(_BF16))
        kdec = kk_b[:, lanes] * jnp.exp2(grp(N_GROUPS - 1, lanes)).astype(_BF16)
        upd = lax.dot_general(v_b, kdec, (((0,), (0,)), ((), ())), preferred_element_type=_F32)
        st_ref[p] = st * decay_tot[:, lanes] + jnp.where(same_head, upd, 0.0)
        outs.append(o_inter + o_intra)
        yield
    o = jnp.concatenate(outs, axis=1)

    ms = _dot(jnp.concatenate(_split_bf16(o * o, 2), axis=1), avg_ref[...])
    gate = pa_ref[0, rows, 2 * A_WIDTH:3 * A_WIDTH].astype(_F32)
    return o * lax.rsqrt(ms + EPS) * gain_ref[...] * _silu(gate)


def _attn_block(rows, n, pb_ref, bias_ref, sink_ref, kv_ref, meta_ref):
    kv_cur = pb_ref[0, rows, B_WIDTH:B_WIDTH + 2 * B_KV_WIDTH]

    kv_all = jnp.concatenate([kv_ref[(n + 1) % 2], kv_cur, meta_ref[...]], axis=0)
    kv_ref[n % 2] = kv_cur
    k_all, v_all = kv_all[:, :B_KV_WIDTH], kv_all[:, B_KV_WIDTH:]
    n_keys = 2 * BLK + N_META

    lane1 = lax.broadcasted_iota(jnp.int32, (1, LANE), 1)
    lo_b = (lane1 < B_HEAD_DIM).astype(_BF16)
    hi_b = (lane1 >= B_HEAD_DIM).astype(_BF16)
    n_col = B_Q_HEADS // B_KV_HEADS
    q_cat = jnp.concatenate([pb_ref[0, rows, c * LANE:(c + 1) * LANE] * (lo_b if j == 0 else hi_b)
                             for j in range(B_KV_HEADS) for c in range(n_col)], axis=0)
    sink = jnp.concatenate([jnp.full((1, BLK), sink_ref[h], _F32) for h in range(B_Q_HEADS)], axis=1)
    yield
    bias = bias_ref[jnp.minimum(n, 2)]
    s = _dot_nt(k_all, q_cat) + jnp.concatenate([bias] * B_Q_HEADS, axis=1)
    yield
    mx = jnp.maximum(jnp.max(s, axis=0, keepdims=True), sink)
    yield
    pr = jnp.exp(s - mx).astype(_BF16)
    yield
    num = lax.dot_general(v_all, pr, (((0,), (0,)), ((), ())), preferred_element_type=_F32)
    den = _dot(jnp.ones((16, n_keys), _BF16), pr)[0:1, :] + jnp.exp(sink - mx)
    yield
    o = num / den
    yield
    half = B_HEAD_DIM
    outs = [jnp.concatenate([o[:half, c * BLK:(c + 1) * BLK],
                             o[half:, (n_col + c) * BLK:(n_col + c + 1) * BLK]], axis=0).T
            for c in range(n_col)]
    return jnp.concatenate(outs, axis=1)


def _conv_block(rows, pc_ref, dww_ref, dwb_ref, lng_ref, lnb_ref, tap_ref, hist_ref):
    u = pc_ref[0, rows, :].astype(_F32)
    hist_ref[HALO:HIST_ROWS, :] = u[:, :C_CHANNELS] * _sigmoid(u[:, C_CHANNELS:])
    hist = hist_ref[...]
    hist_ref[0:HALO, :] = hist[BLK:HIST_ROWS, :]
    yield
    hist_b = hist.astype(_BF16)
    parts = []
    for j in range(C_CONV_WIDTH):
        parts.append(hist_b * dww_ref[j])
        if j % 4 == 3:
            yield
    acc = _dot(tap_ref[...], jnp.concatenate(parts, axis=0))
    yield
    hcv = acc + dwb_ref[...]
    mu = jnp.mean(hcv, axis=-1, keepdims=True)
    ctr = hcv - mu
    yield
    var = jnp.mean(ctr * ctr, axis=-1, keepdims=True)
    hn = ctr * lax.rsqrt(var + EPS) * lng_ref[...] + lnb_ref[...]
    return _silu(hn)


def _interleave(*gens):
    results = [None] * len(gens)
    active = list(enumerate(gens))
    while active:
        still = []
        for k, g in active:
            try:
                next(g)
                still.append((k, g))
            except StopIteration as stop:
                results[k] = stop.value
        active = still
        yield
    return results


def _drain(gen):
    try:
        while True:
            next(gen)
    except StopIteration as stop:
        return stop.value


def _mixer_tile(i, pf_ref, pa_ref, pb_ref, pc_ref, lbc_ref, gain_ref, sink_ref, dww_ref, dwb_ref,
                lng_ref, lnb_ref, cdec_ref, lvl_ref, avg_ref, bias_ref, tap_ref,
                y_ref, st_ref, r_ref, kv_ref, meta_ref, hist_ref):
    for blk in range(TILE_BLOCKS):
        rows = slice(blk * BLK, (blk + 1) * BLK)
        n = i * TILE_BLOCKS + blk
        y_a, y_b, y_c = yield from _interleave(
            _hgrn_block(rows, pf_ref, pa_ref, lbc_ref, gain_ref, cdec_ref, lvl_ref, avg_ref, st_ref, r_ref),
            _attn_block(rows, n, pb_ref, bias_ref, sink_ref, kv_ref, meta_ref),
            _conv_block(rows, pc_ref, dww_ref, dwb_ref, lng_ref, lnb_ref, tap_ref, hist_ref))
        y = jnp.concatenate([y_a, y_b, y_c], axis=1)
        if blk == 0:
            row = lax.broadcasted_iota(jnp.int32, y.shape, 0)
            y = jnp.where((i > 0) | (row >= PAD), y, 0.0)
        y_ref[0, rows, :] = y.astype(_BF16)
        yield


def _mixer_kernel(pf_ref, pa_ref, pb_ref, pc_ref, lbc_ref, gain_ref, sink_ref, dww_ref, dwb_ref,
                  lng_ref, lnb_ref, cdec_ref, lvl_ref, avg_ref, bias_ref, tap_ref,
                  y_ref, st_ref, r_ref, kv_ref, meta_ref, hist_ref):
    i = pl.program_id(1)

    @pl.when(i == 0)
    def _():
        st_ref[...] = jnp.zeros_like(st_ref)
        hist_ref[0:HALO, :] = jnp.zeros((HALO, C_CHANNELS), _F32)
        kv_first = pb_ref[0, 0:BLK, B_WIDTH:B_WIDTH + 2 * B_KV_WIDTH]
        meta_ref[...] = kv_first[PAD:BLK, :]
        kv_ref[1] = kv_first

    _drain(_mixer_tile(i, pf_ref, pa_ref, pb_ref, pc_ref, lbc_ref, gain_ref, sink_ref, dww_ref, dwb_ref,
                       lng_ref, lnb_ref, cdec_ref, lvl_ref, avg_ref, bias_ref, tap_ref,
                       y_ref, st_ref, r_ref, kv_ref, meta_ref, hist_ref))


def _mixer_call(pf, pa, pb, pc, lbc, gain, sinks, dww, dwb, lng, lnb):
    bsz, lp, _ = pf.shape
    tile = TILE_BLOCKS * BLK
    assert lp % tile == 0
    cur = lambda w: pl.BlockSpec((1, tile, w), lambda b, i: (b, i, 0))
    const = lambda *shape: pl.BlockSpec(shape, lambda b, i: (0,) * len(shape))
    cdec = jnp.asarray(_decay_matrix(), _BF16)
    lvl = jnp.asarray(_level_matrix(), _BF16)
    avg = jnp.asarray(_head_mean_matrix(), _BF16)
    bias = jnp.asarray(_attn_bias())
    tap = jnp.asarray(_conv_tap_matrix(), _BF16)
    dww = jnp.broadcast_to(dww.astype(_BF16)[:, None, :], (C_CONV_WIDTH, HIST_ROWS, C_CHANNELS))
    return pl.pallas_call(
        _mixer_kernel,
        grid=(bsz, lp // tile),
        in_specs=[cur(A_WIDTH), cur(PA_WIDTH), cur(PB_WIDTH), cur(PC_WIDTH),
                  const(3, A_WIDTH), const(1, A_WIDTH), pl.BlockSpec(memory_space=pltpu.SMEM),
                  const(C_CONV_WIDTH, HIST_ROWS, C_CHANNELS), const(1, C_CHANNELS), const(1, C_CHANNELS),
                  const(1, C_CHANNELS),
                  const(*cdec.shape), const(*lvl.shape), const(*avg.shape), const(*bias.shape),
                  const(*tap.shape)],
        out_specs=pl.BlockSpec((1, tile, D_MODEL), lambda b, i: (b, i, 0)),
        out_shape=jax.ShapeDtypeStruct((bsz, lp, D_MODEL), _BF16),
        scratch_shapes=[pltpu.VMEM((A_HEADS // 2, LANE, LANE), _F32),
                        pltpu.VMEM((N_GROUPS * BLK, A_WIDTH), _F32),
                        pltpu.VMEM((2, BLK, 2 * B_KV_WIDTH), _BF16),
                        pltpu.VMEM((N_META, 2 * B_KV_WIDTH), _BF16),
                        pltpu.VMEM((HIST_ROWS, C_CHANNELS), _F32)],
        compiler_params=pltpu.CompilerParams(dimension_semantics=("parallel", "arbitrary"),
                                             vmem_limit_bytes=VMEM_LIMIT_BYTES),
        name="mixer",
    )(pf, pa, pb, pc, lbc, gain, sinks, dww, dwb, lng, lnb, cdec, lvl, avg, bias, tap)


def _bq_perm():
    idx = []
    for c in range(B_Q_HEADS // B_KV_HEADS):
        for j in range(B_KV_HEADS):
            head = c + (B_Q_HEADS // B_KV_HEADS) * j
            idx.extend(range(head * B_HEAD_DIM, (head + 1) * B_HEAD_DIM))
    return np.asarray(idx)


def _win_columns():
    q0, f0, i0, g0 = 0, A_WIDTH, 2 * A_WIDTH, 3 * A_WIDTH
    b0 = 4 * A_WIDTH
    bk0 = b0 + B_WIDTH
    c0 = bk0 + 2 * B_KV_WIDTH
    r = lambda s, w: np.arange(s, s + w)
    return np.concatenate([r(f0, A_WIDTH), r(q0, A_WIDTH), r(i0, A_WIDTH), r(g0, A_WIDTH),
                           b0 + _bq_perm(), r(bk0, 2 * B_KV_WIDTH), r(c0, PC_WIDTH)])


def _wout_rows():
    return np.concatenate([np.arange(A_WIDTH), A_WIDTH + _bq_perm(),
                           np.arange(A_WIDTH + B_WIDTH, D_MODEL)])


def kernel(x, meta_tokens, ffn1_norm, ffn1_w_gate, ffn1_w_up, ffn1_w_down, mix_norm, w_in, w_out,
           hgrn_lb_logits, hgrn_out_norm, attn_sinks, conv_dw_w, conv_dw_b, conv_ln_g, conv_ln_b,
           ffn2_norm, ffn2_w_gate, ffn2_w_up, ffn2_w_down, final_norm):
    bsz, seq, _ = x.shape
    depth = w_in.shape[0]
    lp = PAD + N_META + seq
    assert lp % (TILE_BLOCKS * BLK) == 0

    meta = jnp.broadcast_to(meta_tokens[None].astype(x.dtype), (bsz, N_META, D_MODEL))
    h = jnp.concatenate([jnp.zeros((bsz, PAD, D_MODEL), x.dtype), meta, x], axis=1)
    h = h.reshape(bsz * lp, D_MODEL)

    lbs = jnp.cumsum(jax.nn.softmax(hgrn_lb_logits.astype(_F32), axis=0), axis=0)
    lbs = lbs - lbs[0]
    lbc = jnp.stack([jnp.log(lbs), jnp.log1p(-lbs), 1.0 - lbs], axis=1)
    gain = jnp.tile(hgrn_out_norm.astype(_F32), (1, A_HEADS))[:, None, :]
    sinks = attn_sinks.astype(_F32)

    win_cols = _win_columns()
    wout_rows = _wout_rows()
    row2 = lambda a: a.astype(_F32)[None, :]
    qscale = np.ones((IN_WIDTH,), np.float32)
    qscale[A_WIDTH + PA_WIDTH:A_WIDTH + PA_WIDTH + B_WIDTH] = B_HEAD_DIM ** -0.5

    tm = 512 if (bsz * lp) % 512 == 0 else BLK
    for l in range(depth):
        w1 = (ffn1_w_gate[l].astype(_BF16), ffn1_w_up[l].astype(_BF16), ffn1_w_down[l].astype(_BF16))
        w2 = (ffn2_w_gate[l].astype(_BF16), ffn2_w_up[l].astype(_BF16), ffn2_w_down[l].astype(_BF16))
        win = (w_in[l][:, win_cols] * qscale).astype(_BF16)
        wout = w_out[l][wout_rows, :].astype(_BF16)

        h, pf, pa, pb, pc = _ffn_call(h, row2(ffn1_norm[l]), *w1, post=(row2(mix_norm[l]), win), tm=tm)
        shp = lambda a: a.reshape(bsz, lp, a.shape[-1])
        y = _mixer_call(shp(pf), shp(pa), shp(pb), shp(pc), lbc[l], gain[l], sinks[l],
                        conv_dw_w[l].astype(_F32), row2(conv_dw_b[l]), row2(conv_ln_g[l]),
                        row2(conv_ln_b[l]))
        y = y.reshape(bsz * lp, D_MODEL)
        last = l == depth - 1
        res = _ffn_call(h, row2(ffn2_norm[l]), *w2, pre=(y, wout),
                        final_g=row2(final_norm) if last else None, tm=tm)
        h = res[0]
    return h.reshape(bsz, lp, D_MODEL)[:, PAD + N_META:]
```

```python
import functools

import numpy as np
import jax
import jax.numpy as jnp
from jax import lax
from jax.experimental import pallas as pl
from jax.experimental.pallas import tpu as pltpu

D_MODEL = 1024
D_FF = 2816
N_META = 16
EPS = 1e-6
LOG2_E = 1.4426950408889634
FFN_HALF = 0.5

A_HEADS = 6
A_DK = 64
A_WIDTH = 384
B_HEAD_DIM = 64
B_WIDTH = 384
B_Q_HEADS = 6
B_KV_HEADS = 2
B_KV_WIDTH = B_KV_HEADS * B_HEAD_DIM
B_WINDOW = 128
C_CHANNELS = 256
C_CONV_WIDTH = 31
IN_WIDTH = 2688

BLK = 128
PAD = BLK - N_META
LANE = 128
HALO = 32

PA_WIDTH = 3 * A_WIDTH
PB_WIDTH = B_WIDTH + 2 * B_KV_WIDTH
PC_WIDTH = 2 * C_CHANNELS

VMEM_LIMIT_BYTES = 56 * 1024 * 1024

_F32 = jnp.float32
_BF16 = jnp.bfloat16


def _dot(a, b):
    return jnp.dot(a, b, preferred_element_type=_F32)


def _dot_nt(a, b):
    return lax.dot_general(a, b, (((1,), (1,)), ((), ())), preferred_element_type=_F32)


def _rms(x, g):
    ms = jnp.mean(x * x, axis=-1, keepdims=True)
    return x * lax.rsqrt(ms + EPS) * g


def _silu(x):
    return x * (1.0 / (1.0 + jnp.exp(-x)))


def _sigmoid(x):
    return 1.0 / (1.0 + jnp.exp(-x))


def _ffn_kernel(*refs, pre, post, final, ff_chunk):
    it = iter(refs)
    h_ref = next(it)
    if pre:
        y_ref, wout_ref = next(it), next(it)
    g_ref, wg_ref, wu_ref, wd_ref = next(it), next(it), next(it), next(it)
    if post:
        g2_ref, win_ref = next(it), next(it)
    if final:
        gf_ref = next(it)
    hout_ref = next(it)
    if post:
        pf_ref, pa_ref, pb_ref, pc_ref = next(it), next(it), next(it), next(it)
    act_ref = next(it)

    h = h_ref[...]
    if pre:
        h = h + _dot(y_ref[...], wout_ref[...])
    xn = _rms(h, g_ref[...]).astype(_BF16)
    for start in range(0, D_FF, ff_chunk):
        cols = slice(start, min(start + ff_chunk, D_FF))
        gate = _dot(xn, wg_ref[:, cols])
        up = _dot(xn, wu_ref[:, cols])
        act_ref[:, cols] = (_silu(gate) * up).astype(_BF16)
    h = h + FFN_HALF * _dot(act_ref[...], wd_ref[...])
    if final:
        hout_ref[...] = _rms(h, gf_ref[...])
    else:
        hout_ref[...] = h
    if post:
        xn2 = _rms(h, g2_ref[...]).astype(_BF16)
        pf_ref[...] = _dot(xn2, win_ref[:, 0:A_WIDTH])
        o = A_WIDTH
        pa_ref[...] = _dot(xn2, win_ref[:, o:o + PA_WIDTH]).astype(_BF16)
        o += PA_WIDTH
        pb_ref[...] = _dot(xn2, win_ref[:, o:o + PB_WIDTH]).astype(_BF16)
        o += PB_WIDTH
        pc_ref[...] = _dot(xn2, win_ref[:, o:o + PC_WIDTH]).astype(_BF16)


def _resident(shape):
    return pl.BlockSpec(shape, lambda i: (0,) * len(shape), pipeline_mode=pl.Buffered(1))


def _ffn_call(h, norm_g, wg, wu, wd, *, pre=None, post=None, final_g=None, tm=512, ff_chunk=768):
    rows = h.shape[0]
    assert rows % tm == 0
    row_spec = lambda w: pl.BlockSpec((tm, w), lambda i: (i, 0))
    args, specs = [h], [row_spec(D_MODEL)]
    if pre is not None:
        y, wout = pre
        args += [y, wout]
        specs += [row_spec(D_MODEL), _resident((D_MODEL, D_MODEL))]
    args += [norm_g, wg, wu, wd]
    specs += [_resident((1, D_MODEL)), _resident((D_MODEL, D_FF)), _resident((D_MODEL, D_FF)),
              _resident((D_FF, D_MODEL))]
    if post is not None:
        g2, win = post
        args += [g2, win]
        specs += [_resident((1, D_MODEL)), _resident((D_MODEL, IN_WIDTH))]
    if final_g is not None:
        args += [final_g]
        specs += [_resident((1, D_MODEL))]
    out_shape = [jax.ShapeDtypeStruct((rows, D_MODEL), _F32)]
    out_specs = [row_spec(D_MODEL)]
    if post is not None:
        for w, dt in ((A_WIDTH, _F32), (PA_WIDTH, _BF16), (PB_WIDTH, _BF16), (PC_WIDTH, _BF16)):
            out_shape.append(jax.ShapeDtypeStruct((rows, w), dt))
            out_specs.append(row_spec(w))
    kern = functools.partial(_ffn_kernel, pre=pre is not None, post=post is not None,
                             final=final_g is not None, ff_chunk=ff_chunk)
    return pl.pallas_call(
        kern,
        grid=(rows // tm,),
        in_specs=specs,
        out_specs=out_specs,
        out_shape=out_shape,
        scratch_shapes=[pltpu.VMEM((tm, D_FF), _BF16)],
        compiler_params=pltpu.CompilerParams(dimension_semantics=("parallel",),
                                             vmem_limit_bytes=VMEM_LIMIT_BYTES),
        name="ffn",
    )(*args)


N_LEVELS = 7
N_GROUPS = N_LEVELS + 2
DECAY_TERMS = 2
HIST_ROWS = HALO + BLK
TILE_BLOCKS = 3


def _split_bf16(x, terms):
    out = []
    for _ in range(terms - 1):
        part = x.astype(_BF16)
        out.append(part)
        x = x - part.astype(_F32)
    out.append(x.astype(_BF16))
    return out


def _decay_matrix():
    c = np.zeros((N_GROUPS, BLK, BLK), np.float32)
    for t in range(BLK):
        c[0, t, :t + 1] = 1.0
        for l in range(1, N_LEVELS + 1):
            m = 1 << (l - 1)
            ref = (t // (2 * m)) * 2 * m + m - 1
            if t > ref:
                c[l, t, ref + 1:t + 1] = 1.0
            else:
                c[l, t, t + 1:ref + 1] = 1.0
        c[N_GROUPS - 1, t, t + 1:] = 1.0
    c = c.reshape(N_GROUPS * BLK, BLK)
    return np.concatenate([c] * DECAY_TERMS, axis=1)


def _level_matrix():
    t = np.arange(BLK)[:, None]
    s = np.arange(BLK)[None, :]
    x = t ^ s
    lvl = np.where(x == 0, 0, np.floor(np.log2(np.maximum(x, 1))) + 1)
    lvl = np.where(s > t, -1, lvl).astype(np.float32)
    return np.concatenate([lvl, lvl], axis=1)


def _head_mean_matrix():
    h = np.arange(A_WIDTH) // A_DK
    m = np.where(h[:, None] == h[None, :], 1.0 / A_DK, 0.0).astype(np.float32)
    return np.concatenate([m, m], axis=0)


def _conv_tap_matrix():
    first = HALO - (C_CONV_WIDTH - 1)
    m = np.zeros((BLK, C_CONV_WIDTH, HIST_ROWS), np.float32)
    for t in range(BLK):
        for j in range(C_CONV_WIDTH):
            m[t, j, t + first + j] = 1.0
    return m.reshape(BLK, C_CONV_WIDTH * HIST_ROWS)


def _attn_bias():
    n_keys = 2 * BLK + N_META
    qi = np.arange(BLK)[:, None]
    kj = np.arange(n_keys)[None, :]
    out = []
    for n in range(3):
        prev_ok = (kj < BLK) & (kj > qi) & (n >= 2)
        cur_ok = (kj >= BLK) & (kj < 2 * BLK) & (kj - BLK <= qi) & (n >= 1)
        meta_ok = (kj >= 2 * BLK) & ((n >= 1) | (kj - 2 * BLK + PAD <= qi))
        out.append(np.where(prev_ok | cur_ok | meta_ok, 0.0, -1e30).T)
    return np.stack(out).astype(np.float32)


def _hgrn_block(rows, pf_ref, pa_ref, lbc_ref, gain_ref, cdec_ref, lvl_ref, avg_ref, carry, r_ref):
    row = lax.broadcasted_iota(jnp.int32, (BLK, LANE), 0)
    col = lax.broadcasted_iota(jnp.int32, (BLK, LANE), 1)
    same_head = (row < A_DK) == (col < A_DK)
    lane1 = lax.broadcasted_iota(jnp.int32, (1, LANE), 1)
    lo_f = (lane1 < A_DK).astype(_F32)
    hi_f = 1.0 - lo_f
    lo_b, hi_b = lo_f.astype(_BF16), hi_f.astype(_BF16)

    z = pf_ref[0, rows, :]
    log_lb, log_1m_lb, one_m_lb = lbc_ref[0:1, :], lbc_ref[1:2, :], lbc_ref[2:3, :]
    soft = jnp.log(1.0 + jnp.exp(-jnp.abs(z)))
    log_sig = jnp.minimum(z, 0.0) - soft
    log_sig_neg = jnp.minimum(-z, 0.0) - soft
    b = log_1m_lb + log_sig
    log_f = jnp.maximum(log_lb, b) + jnp.log(1.0 + jnp.exp(-jnp.abs(log_lb - b)))
    kk = one_m_lb * jnp.exp(log_sig_neg)
    kk_b = kk.astype(_BF16)
    yield

    log_f_terms = jnp.concatenate(_split_bf16(log_f * LOG2_E, DECAY_TERMS), axis=0)
    half_rows = N_GROUPS * BLK // 2
    r_ref[0:half_rows, :] = _dot(cdec_ref[0:half_rows, :], log_f_terms)
    r_ref[half_rows:, :] = _dot(cdec_ref[half_rows:, :], log_f_terms)
    grp = lambda g, lanes: r_ref[g * BLK:(g + 1) * BLK, lanes]
    decay_tot = jnp.exp2(r_ref[BLK - 1:BLK, :])
    lvl = lvl_ref[...]
    yield

    outs = []
    for p in range(A_HEADS // 2):
        lanes = slice(p * LANE, (p + 1) * LANE)
        q_b = pa_ref[0, rows, lanes]
        v_b = pa_ref[0, rows, A_WIDTH + p * LANE:A_WIDTH + (p + 1) * LANE]
        k_both = jnp.concatenate([(kk[:, lanes] * lo_f).astype(_BF16),
                                  (kk[:, lanes] * hi_f).astype(_BF16)], axis=0)
        a_cat = jnp.where(lvl == 0.0, _dot_nt(q_b, k_both).astype(_BF16),
                          jnp.zeros((BLK, 2 * BLK), _BF16))
        yield
        for l in range(1, N_LEVELS + 1):
            e = jnp.exp2(grp(l, lanes)).astype(_BF16)
            k_cat = k_both * jnp.concatenate([e, e], axis=0)
            a_cat = jnp.where(lvl == float(l), _dot_nt(q_b * e, k_cat).astype(_BF16), a_cat)
            yield
        v_cat = jnp.concatenate([v_b * lo_b, v_b * hi_b], axis=0)
        o_intra = _dot(a_cat, v_cat)
        yield

        st = carry['st'][p]
        qg = q_b * jnp.exp2(grp(0, lanes)).astype(_BF16)
        o_inter = _dot_nt(qg, st.astype(_BF16))
        kdec = kk_b[:, lanes] * jnp.exp2(grp(N_GROUPS - 1, lanes)).astype(_BF16)
        upd = lax.dot_general(v_b, kdec, (((0,), (0,)), ((), ())), preferred_element_type=_F32)
        carry['st'][p] = st * decay_tot[:, lanes] + jnp.where(same_head, upd, 0.0)
        outs.append(o_inter + o_intra)
        yield
    o = jnp.concatenate(outs, axis=1)

    ms = _dot(jnp.concatenate(_split_bf16(o * o, 2), axis=1), avg_ref[...])
    gate = pa_ref[0, rows, 2 * A_WIDTH:3 * A_WIDTH].astype(_F32)
    return o * lax.rsqrt(ms + EPS) * gain_ref[...] * _silu(gate)


def _attn_block(rows, n, pb_ref, bias_ref, sink_ref, carry, pbm_ref):
    kv_cur = pb_ref[0, rows, B_WIDTH:B_WIDTH + 2 * B_KV_WIDTH]

    kv_meta = pbm_ref[0, PAD:BLK, B_WIDTH:B_WIDTH + 2 * B_KV_WIDTH]
    kv_all = jnp.concatenate([carry['kv'], kv_cur, kv_meta], axis=0)
    carry['kv'] = kv_cur
    k_all, v_all = kv_all[:, :B_KV_WIDTH], kv_all[:, B_KV_WIDTH:]
    n_keys = 2 * BLK + N_META

    lane1 = lax.broadcasted_iota(jnp.int32, (1, LANE), 1)
    lo_b = (lane1 < B_HEAD_DIM).astype(_BF16)
    hi_b = (lane1 >= B_HEAD_DIM).astype(_BF16)
    n_col = B_Q_HEADS // B_KV_HEADS
    q_cat = jnp.concatenate([pb_ref[0, rows, c * LANE:(c + 1) * LANE] * (lo_b if j == 0 else hi_b)
                             for j in range(B_KV_HEADS) for c in range(n_col)], axis=0)
    sink = jnp.concatenate([jnp.full((1, BLK), sink_ref[h], _F32) for h in range(B_Q_HEADS)], axis=1)
    yield
    bias = bias_ref[jnp.minimum(n, 2)]
    s = _dot_nt(k_all, q_cat) + jnp.concatenate([bias] * B_Q_HEADS, axis=1)
    yield
    mx = jnp.maximum(jnp.max(s, axis=0, keepdims=True), sink)
    yield
    pr = jnp.exp(s - mx).astype(_BF16)
    yield
    num = lax.dot_general(v_all, pr, (((0,), (0,)), ((), ())), preferred_element_type=_F32)
    den = _dot(jnp.ones((16, n_keys), _BF16), pr)[0:1, :] + jnp.exp(sink - mx)
    yield
    o = num / den
    yield
    half = B_HEAD_DIM
    outs = [jnp.concatenate([o[:half, c * BLK:(c + 1) * BLK],
                             o[half:, (n_col + c) * BLK:(n_col + c + 1) * BLK]], axis=0).T
            for c in range(n_col)]
    return jnp.concatenate(outs, axis=1)


def _conv_block(rows, pc_ref, dww_ref, dwb_ref, lng_ref, lnb_ref, tap_ref, carry):
    u = pc_ref[0, rows, :].astype(_F32)
    glu = u[:, :C_CHANNELS] * _sigmoid(u[:, C_CHANNELS:])
    hist = jnp.concatenate([carry['tail'], glu], axis=0)
    carry['tail'] = glu[BLK - HALO:, :]
    yield
    hist_b = hist.astype(_BF16)
    parts = []
    for j in range(C_CONV_WIDTH):
        parts.append(hist_b * dww_ref[j])
        if j % 4 == 3:
            yield
    acc = _dot(tap_ref[...], jnp.concatenate(parts, axis=0))
    yield
    hcv = acc + dwb_ref[...]
    mu = jnp.mean(hcv, axis=-1, keepdims=True)
    ctr = hcv - mu
    yield
    var = jnp.mean(ctr * ctr, axis=-1, keepdims=True)
    hn = ctr * lax.rsqrt(var + EPS) * lng_ref[...] + lnb_ref[...]
    return _silu(hn)


def _interleave(*gens):
    results = [None] * len(gens)
    active = list(enumerate(gens))
    while active:
        still = []
        for k, g in active:
            try:
                next(g)
                still.append((k, g))
            except StopIteration as stop:
                results[k] = stop.value
        active = still
        yield
    return results


def _drain(gen):
    try:
        while True:
            next(gen)
    except StopIteration as stop:
        return stop.value


def _mixer_tile(i, pf_ref, pa_ref, pb_ref, pbp_ref, pbm_ref, pc_ref, pcp_ref, lbc_ref, gain_ref, sink_ref,
                dww_ref, dwb_ref, lng_ref, lnb_ref, cdec_ref, lvl_ref, avg_ref, bias_ref, tap_ref,
                y_ref, st_ref, r_ref):
    u_prev = pcp_ref[0, BLK - HALO:BLK, :].astype(_F32)
    tail = u_prev[:, :C_CHANNELS] * _sigmoid(u_prev[:, C_CHANNELS:])
    carry = {'st': [st_ref[p] for p in range(A_HEADS // 2)],
             'kv': pbp_ref[0, :, B_WIDTH:B_WIDTH + 2 * B_KV_WIDTH],
             'tail': jnp.where(i > 0, tail, 0.0)}
    gens = []
    for blk in range(TILE_BLOCKS):
        rows = slice(blk * BLK, (blk + 1) * BLK)
        n = i * TILE_BLOCKS + blk
        gens += [_hgrn_block(rows, pf_ref, pa_ref, lbc_ref, gain_ref, cdec_ref, lvl_ref, avg_ref, carry,
                             r_ref.at[blk]),
                 _attn_block(rows, n, pb_ref, bias_ref, sink_ref, carry, pbm_ref),
                 _conv_block(rows, pc_ref, dww_ref, dwb_ref, lng_ref, lnb_ref, tap_ref, carry)]
    res = yield from _interleave(*gens)
    for blk in range(TILE_BLOCKS):
        rows = slice(blk * BLK, (blk + 1) * BLK)
        y = jnp.concatenate(res[3 * blk:3 * blk + 3], axis=1)
        if blk == 0:
            row = lax.broadcasted_iota(jnp.int32, y.shape, 0)
            y = jnp.where((i > 0) | (row >= PAD), y, 0.0)
        y_ref[0, rows, :] = y.astype(_BF16)
    for p in range(A_HEADS // 2):
        st_ref[p] = carry['st'][p]


def _mixer_kernel(pf_ref, pa_ref, pb_ref, pbp_ref, pbm_ref, pc_ref, pcp_ref, lbc_ref, gain_ref, sink_ref,
                  dww_ref, dwb_ref, lng_ref, lnb_ref, cdec_ref, lvl_ref, avg_ref, bias_ref, tap_ref,
                  y_ref, st_ref, r_ref):
    i = pl.program_id(1)

    @pl.when(i == 0)
    def _():
        st_ref[...] = jnp.zeros_like(st_ref)

    _drain(_mixer_tile(i, pf_ref, pa_ref, pb_ref, pbp_ref, pbm_ref, pc_ref, pcp_ref, lbc_ref, gain_ref,
                       sink_ref, dww_ref, dwb_ref, lng_ref, lnb_ref, cdec_ref, lvl_ref, avg_ref, bias_ref,
                       tap_ref, y_ref, st_ref, r_ref))


def _mixer_call(pf, pa, pb, pc, lbc, gain, sinks, dww, dwb, lng, lnb):
    bsz, lp, _ = pf.shape
    tile = TILE_BLOCKS * BLK
    assert lp % tile == 0
    cur = lambda w: pl.BlockSpec((1, tile, w), lambda b, i: (b, i, 0))
    prev = lambda w: pl.BlockSpec((1, BLK, w), lambda b, i: (b, jnp.maximum(i * TILE_BLOCKS - 1, 0), 0))
    first = lambda w: pl.BlockSpec((1, BLK, w), lambda b, i: (b, 0, 0))
    const = lambda *shape: pl.BlockSpec(shape, lambda b, i: (0,) * len(shape))
    cdec = jnp.asarray(_decay_matrix(), _BF16)
    lvl = jnp.asarray(_level_matrix(), _BF16)
    avg = jnp.asarray(_head_mean_matrix(), _BF16)
    bias = jnp.asarray(_attn_bias())
    tap = jnp.asarray(_conv_tap_matrix(), _BF16)
    dww = jnp.broadcast_to(dww.astype(_BF16)[:, None, :], (C_CONV_WIDTH, HIST_ROWS, C_CHANNELS))
    return pl.pallas_call(
        _mixer_kernel,
        grid=(bsz, lp // tile),
        in_specs=[cur(A_WIDTH), cur(PA_WIDTH), cur(PB_WIDTH), prev(PB_WIDTH), first(PB_WIDTH),
                  cur(PC_WIDTH), prev(PC_WIDTH),
                  const(3, A_WIDTH), const(1, A_WIDTH), pl.BlockSpec(memory_space=pltpu.SMEM),
                  const(C_CONV_WIDTH, HIST_ROWS, C_CHANNELS), const(1, C_CHANNELS), const(1, C_CHANNELS),
                  const(1, C_CHANNELS),
                  const(*cdec.shape), const(*lvl.shape), const(*avg.shape), const(*bias.shape),
                  const(*tap.shape)],
        out_specs=pl.BlockSpec((1, tile, D_MODEL), lambda b, i: (b, i, 0)),
        out_shape=jax.ShapeDtypeStruct((bsz, lp, D_MODEL), _BF16),
        scratch_shapes=[pltpu.VMEM((A_HEADS // 2, LANE, LANE), _F32),
                        pltpu.VMEM((TILE_BLOCKS, N_GROUPS * BLK, A_WIDTH), _F32)],
        compiler_params=pltpu.CompilerParams(dimension_semantics=("parallel", "arbitrary"),
                                             vmem_limit_bytes=VMEM_LIMIT_BYTES),
        name="mixer",
    )(pf, pa, pb, pb, pb, pc, pc, lbc, gain, sinks, dww, dwb, lng, lnb, cdec, lvl, avg, bias, tap)


def _bq_perm():
    idx = []
    for c in range(B_Q_HEADS // B_KV_HEADS):
        for j in range(B_KV_HEADS):
            head = c + (B_Q_HEADS // B_KV_HEADS) * j
            idx.extend(range(head * B_HEAD_DIM, (head + 1) * B_HEAD_DIM))
    return np.asarray(idx)


def _win_columns():
    q0, f0, i0, g0 = 0, A_WIDTH, 2 * A_WIDTH, 3 * A_WIDTH
    b0 = 4 * A_WIDTH
    bk0 = b0 + B_WIDTH
    c0 = bk0 + 2 * B_KV_WIDTH
    r = lambda s, w: np.arange(s, s + w)
    return np.concatenate([r(f0, A_WIDTH), r(q0, A_WIDTH), r(i0, A_WIDTH), r(g0, A_WIDTH),
                           b0 + _bq_perm(), r(bk0, 2 * B_KV_WIDTH), r(c0, PC_WIDTH)])


def _wout_rows():
    return np.concatenate([np.arange(A_WIDTH), A_WIDTH + _bq_perm(),
                           np.arange(A_WIDTH + B_WIDTH, D_MODEL)])


def kernel(x, meta_tokens, ffn1_norm, ffn1_w_gate, ffn1_w_up, ffn1_w_down, mix_norm, w_in, w_out,
           hgrn_lb_logits, hgrn_out_norm, attn_sinks, conv_dw_w, conv_dw_b, conv_ln_g, conv_ln_b,
           ffn2_norm, ffn2_w_gate, ffn2_w_up, ffn2_w_down, final_norm):
    bsz, seq, _ = x.shape
    depth = w_in.shape[0]
    lp = PAD + N_META + seq
    assert lp % (TILE_BLOCKS * BLK) == 0

    meta = jnp.broadcast_to(meta_tokens[None].astype(x.dtype), (bsz, N_META, D_MODEL))
    h = jnp.concatenate([jnp.zeros((bsz, PAD, D_MODEL), x.dtype), meta, x], axis=1)
    h = h.reshape(bsz * lp, D_MODEL)

    lbs = jnp.cumsum(jax.nn.softmax(hgrn_lb_logits.astype(_F32), axis=0), axis=0)
    lbs = lbs - lbs[0]
    lbc = jnp.stack([jnp.log(lbs), jnp.log1p(-lbs), 1.0 - lbs], axis=1)
    gain = jnp.tile(hgrn_out_norm.astype(_F32), (1, A_HEADS))[:, None, :]
    sinks = attn_sinks.astype(_F32)

    win_cols = _win_columns()
    wout_rows = _wout_rows()
    row2 = lambda a: a.astype(_F32)[None, :]
    qscale = np.ones((IN_WIDTH,), np.float32)
    qscale[A_WIDTH + PA_WIDTH:A_WIDTH + PA_WIDTH + B_WIDTH] = B_HEAD_DIM ** -0.5

    tm = 512 if (bsz * lp) % 512 == 0 else BLK
    for l in range(depth):
        w1 = (ffn1_w_gate[l].astype(_BF16), ffn1_w_up[l].astype(_BF16), ffn1_w_down[l].astype(_BF16))
        w2 = (ffn2_w_gate[l].astype(_BF16), ffn2_w_up[l].astype(_BF16), ffn2_w_down[l].astype(_BF16))
        win = (w_in[l][:, win_cols] * qscale).astype(_BF16)
        wout = w_out[l][wout_rows, :].astype(_BF16)

        h, pf, pa, pb, pc = _ffn_call(h, row2(ffn1_norm[l]), *w1, post=(row2(mix_norm[l]), win), tm=tm)
        shp = lambda a: a.reshape(bsz, lp, a.shape[-1])
        y = _mixer_call(shp(pf), shp(pa), shp(pb), shp(pc), lbc[l], gain[l], sinks[l],
                        conv_dw_w[l].astype(_F32), row2(conv_dw_b[l]), row2(conv_ln_g[l]),
                        row2(conv_ln_b[l]))
        y = y.reshape(bsz * lp, D_MODEL)
        last = l == depth - 1
        res = _ffn_call(h, row2(ffn2_norm[l]), *w2, pre=(y, wout),
                        final_g=row2(final_norm) if last else None, tm=tm)
        h = res[0]
    return h.reshape(bsz, lp, D_MODEL)[:, PAD + N_META:]
```

```python
import functools

import numpy as np
import jax
import jax.numpy as jnp
from jax import lax
from jax.experimental import pallas as pl
from jax.experimental.pallas import tpu as pltpu

D_MODEL = 1024
D_FF = 2816
N_META = 16
EPS = 1e-6
LOG2_E = 1.4426950408889634
FFN_HALF = 0.5

A_HEADS = 6
A_DK = 64
A_WIDTH = 384
B_HEAD_DIM = 64
B_WIDTH = 384
B_Q_HEADS = 6
B_KV_HEADS = 2
B_KV_WIDTH = B_KV_HEADS * B_HEAD_DIM
B_WINDOW = 128
C_CHANNELS = 256
C_CONV_WIDTH = 31
IN_WIDTH = 2688

BLK = 128
PAD = BLK - N_META
LANE = 128
HALO = 32

PA_WIDTH = 3 * A_WIDTH
PB_WIDTH = B_WIDTH + 2 * B_KV_WIDTH
PC_WIDTH = 2 * C_CHANNELS

VMEM_LIMIT_BYTES = 56 * 1024 * 1024

_F32 = jnp.float32
_BF16 = jnp.bfloat16


def _dot(a, b):
    return jnp.dot(a, b, preferred_element_type=_F32)


def _dot_nt(a, b):
    return lax.dot_general(a, b, (((1,), (1,)), ((), ())), preferred_element_type=_F32)


def _rms(x, g):
    ms = jnp.mean(x * x, axis=-1, keepdims=True)
    return x * lax.rsqrt(ms + EPS) * g


def _silu(x):
    return x * (1.0 / (1.0 + jnp.exp(-x)))


def _sigmoid(x):
    return 1.0 / (1.0 + jnp.exp(-x))


def _ffn_kernel(*refs, pre, post, final, ff_chunk):
    it = iter(refs)
    h_ref = next(it)
    if pre:
        y_ref, wout_ref = next(it), next(it)
    g_ref, wg_ref, wu_ref, wd_ref = next(it), next(it), next(it), next(it)
    if post:
        g2_ref, win_ref = next(it), next(it)
    if final:
        gf_ref = next(it)
    hout_ref = next(it)
    if post:
        pf_ref, pa_ref, pb_ref, pc_ref = next(it), next(it), next(it), next(it)
    act_ref = next(it)

    rows = hout_ref.shape[-2]
    h = h_ref[...].reshape(rows, D_MODEL)
    if pre:
        h = h + _dot(y_ref[...].reshape(rows, D_MODEL), wout_ref[...])
    xn = _rms(h, g_ref[...]).astype(_BF16)
    for start in range(0, D_FF, ff_chunk):
        cols = slice(start, min(start + ff_chunk, D_FF))
        gate = _dot(xn, wg_ref[:, cols])
        up = _dot(xn, wu_ref[:, cols])
        act_ref[:, cols] = (_silu(gate) * up).astype(_BF16)
    h = h + FFN_HALF * _dot(act_ref[...], wd_ref[...])
    if final:
        hout_ref[...] = _rms(h, gf_ref[...]).reshape(hout_ref.shape)
    else:
        hout_ref[...] = h
    if post:
        xn2 = _rms(h, g2_ref[...]).astype(_BF16)
        pf_ref[...] = _dot(xn2, win_ref[:, 0:A_WIDTH])
        o = A_WIDTH
        pa_ref[...] = _dot(xn2, win_ref[:, o:o + PA_WIDTH]).astype(_BF16)
        o += PA_WIDTH
        pb_ref[...] = _dot(xn2, win_ref[:, o:o + PB_WIDTH]).astype(_BF16)
        o += PB_WIDTH
        pc_ref[...] = _dot(xn2, win_ref[:, o:o + PC_WIDTH]).astype(_BF16)


def _resident(shape):
    return pl.BlockSpec(shape, lambda i: (0,) * len(shape), pipeline_mode=pl.Buffered(1))


def _ffn_call(h, norm_g, wg, wu, wd, *, pre=None, post=None, final_g=None, tm=512, ff_chunk=768):
    if final_g is not None:
        return _final_ffn_call(h, norm_g, wg, wu, wd, pre, final_g, ff_chunk)
    rows = h.shape[0]
    assert rows % tm == 0
    row_spec = lambda w: pl.BlockSpec((tm, w), lambda i: (i, 0))
    args, specs = [h], [row_spec(D_MODEL)]
    if pre is not None:
        y, wout = pre
        args += [y, wout]
        specs += [row_spec(D_MODEL), _resident((D_MODEL, D_MODEL))]
    args += [norm_g, wg, wu, wd]
    specs += [_resident((1, D_MODEL)), _resident((D_MODEL, D_FF)), _resident((D_MODEL, D_FF)),
              _resident((D_FF, D_MODEL))]
    if post is not None:
        g2, win = post
        args += [g2, win]
        specs += [_resident((1, D_MODEL)), _resident((D_MODEL, IN_WIDTH))]
    if final_g is not None:
        args += [final_g]
        specs += [_resident((1, D_MODEL))]
    out_shape = [jax.ShapeDtypeStruct((rows, D_MODEL), _F32)]
    out_specs = [row_spec(D_MODEL)]
    if post is not None:
        for w, dt in ((A_WIDTH, _F32), (PA_WIDTH, _BF16), (PB_WIDTH, _BF16), (PC_WIDTH, _BF16)):
            out_shape.append(jax.ShapeDtypeStruct((rows, w), dt))
            out_specs.append(row_spec(w))
    kern = functools.partial(_ffn_kernel, pre=pre is not None, post=post is not None,
                             final=final_g is not None, ff_chunk=ff_chunk)
    return pl.pallas_call(
        kern,
        grid=(rows // tm,),
        in_specs=specs,
        out_specs=out_specs,
        out_shape=out_shape,
        scratch_shapes=[pltpu.VMEM((tm, D_FF), _BF16)],
        compiler_params=pltpu.CompilerParams(dimension_semantics=("parallel",),
                                             vmem_limit_bytes=VMEM_LIMIT_BYTES),
        name="ffn",
    )(*args)


def _final_ffn_call(h, norm_g, wg, wu, wd, pre, final_g, ff_chunk):
    y, wout = pre
    bsz, nb = h.shape[0], h.shape[1]
    g = max(k for k in (4, 2, 1) if (nb - 1) % k == 0)
    tile = lambda: pl.BlockSpec((pl.Element(1), pl.Element(g), pl.Element(BLK), pl.Element(D_MODEL)),
                                lambda b, j: (b, 1 + g * j, 0, 0))
    res = lambda *shape: pl.BlockSpec(shape, lambda b, j: (0,) * len(shape), pipeline_mode=pl.Buffered(1))
    kern = functools.partial(_ffn_kernel, pre=True, post=False, final=True, ff_chunk=ff_chunk)
    return pl.pallas_call(
        kern,
        grid=(bsz, (nb - 1) // g),
        in_specs=[tile(), tile(), res(D_MODEL, D_MODEL), res(1, D_MODEL), res(D_MODEL, D_FF),
                  res(D_MODEL, D_FF), res(D_FF, D_MODEL), res(1, D_MODEL)],
        out_specs=pl.BlockSpec((1, g * BLK, D_MODEL), lambda b, j: (b, j, 0)),
        out_shape=jax.ShapeDtypeStruct((bsz, (nb - 1) * BLK, D_MODEL), _F32),
        scratch_shapes=[pltpu.VMEM((g * BLK, D_FF), _BF16)],
        compiler_params=pltpu.CompilerParams(dimension_semantics=("parallel", "parallel"),
                                             vmem_limit_bytes=VMEM_LIMIT_BYTES),
        name="ffn_final",
    )(h, y, wout, norm_g, wg, wu, wd, final_g)


N_LEVELS = 7
MXU_LEVELS = 3
N_GROUPS = MXU_LEVELS + 1
DECAY_TERMS = 2
HIST_ROWS = HALO + BLK
TILE_BLOCKS = 3


def _split_bf16(x, terms):
    out = []
    for _ in range(terms - 1):
        part = x.astype(_BF16)
        out.append(part)
        x = x - part.astype(_F32)
    out.append(x.astype(_BF16))
    return out


def _decay_matrix():
    c = np.zeros((N_GROUPS, BLK, BLK), np.float32)
    for t in range(BLK):
        c[0, t, :t + 1] = 1.0
        for l in range(1, MXU_LEVELS + 1):
            m = 1 << (l - 1)
            ref = (t // (2 * m)) * 2 * m + m - 1
            if t > ref:
                c[l, t, ref + 1:t + 1] = 1.0
            else:
                c[l, t, t + 1:ref + 1] = 1.0
    c = c.reshape(N_GROUPS * BLK, BLK)
    return np.concatenate([c] * DECAY_TERMS, axis=1)


def _level_matrix():
    t = np.arange(BLK)[:, None]
    s = np.arange(BLK)[None, :]
    x = t ^ s
    lvl = np.where(x == 0, 0, np.floor(np.log2(np.maximum(x, 1))) + 1)
    lvl = np.where(s > t, -1, lvl).astype(np.float32)
    return np.concatenate([lvl, lvl], axis=1)


def _head_mean_matrix():
    h = np.arange(A_WIDTH) // A_DK
    m = np.where(h[:, None] == h[None, :], 1.0 / A_DK, 0.0).astype(np.float32)
    return np.concatenate([m, m], axis=0)


def _conv_tap_matrix():
    first = HALO - (C_CONV_WIDTH - 1)
    m = np.zeros((BLK, C_CONV_WIDTH, HIST_ROWS), np.float32)
    for t in range(BLK):
        for j in range(C_CONV_WIDTH):
            m[t, j, t + first + j] = 1.0
    return m.reshape(BLK, C_CONV_WIDTH * HIST_ROWS)


def _attn_bias():
    n_keys = 2 * BLK + N_META
    qi = np.arange(BLK)[:, None]
    kj = np.arange(n_keys)[None, :]
    out = []
    for n in range(3):
        prev_ok = (kj < BLK) & (kj > qi) & (n >= 2)
        cur_ok = (kj >= BLK) & (kj < 2 * BLK) & (kj - BLK <= qi) & (n >= 1)
        meta_ok = (kj >= 2 * BLK) & ((n >= 1) | (kj - 2 * BLK + PAD <= qi))
        out.append(np.where(prev_ok | cur_ok | meta_ok, 0.0, -1e30).T)
    return np.stack(out).astype(np.float32)


def _level_decay(g_p, m):
    ref = jnp.concatenate([jnp.broadcast_to(g_p[r:r + 1, :], (2 * m, g_p.shape[1]))
                           for r in range(m - 1, BLK, 2 * m)], axis=0)
    return -jnp.abs(g_p - ref)


def _hgrn_block(rows, pf_ref, pa_ref, lbc_ref, gain_ref, cdec_ref, lvl_ref, avg_ref, carry, r_ref):
    row = lax.broadcasted_iota(jnp.int32, (BLK, LANE), 0)
    col = lax.broadcasted_iota(jnp.int32, (BLK, LANE), 1)
    same_head = (row < A_DK) == (col < A_DK)
    lane1 = lax.broadcasted_iota(jnp.int32, (1, LANE), 1)
    lo_f = (lane1 < A_DK).astype(_F32)
    hi_f = 1.0 - lo_f
    lo_b, hi_b = lo_f.astype(_BF16), hi_f.astype(_BF16)

    z = pf_ref[0, rows, :]
    log_lb, log_1m_lb, one_m_lb = lbc_ref[0:1, :], lbc_ref[1:2, :], lbc_ref[2:3, :]
    soft = jnp.log(1.0 + jnp.exp(-jnp.abs(z)))
    log_sig = jnp.minimum(z, 0.0) - soft
    log_sig_neg = jnp.minimum(-z, 0.0) - soft
    b = log_1m_lb + log_sig
    log_f = jnp.maximum(log_lb, b) + jnp.log(1.0 + jnp.exp(-jnp.abs(log_lb - b)))
    kk = one_m_lb * jnp.exp(log_sig_neg)
    kk_b = kk.astype(_BF16)
    yield

    log_f_terms = jnp.concatenate(_split_bf16(log_f * LOG2_E, DECAY_TERMS), axis=0)
    half_rows = N_GROUPS * BLK // 2
    r_ref[0:half_rows, :] = _dot(cdec_ref[0:half_rows, :], log_f_terms)
    r_ref[half_rows:, :] = _dot(cdec_ref[half_rows:, :], log_f_terms)
    grp = lambda g, lanes: r_ref[g * BLK:(g + 1) * BLK, lanes]
    decay_tot = jnp.exp2(r_ref[BLK - 1:BLK, :])
    lvl = lvl_ref[...]
    yield

    outs = []
    for p in range(A_HEADS // 2):
        lanes = slice(p * LANE, (p + 1) * LANE)
        q_b = pa_ref[0, rows, lanes]
        g_p = grp(0, lanes)
        v_b = pa_ref[0, rows, A_WIDTH + p * LANE:A_WIDTH + (p + 1) * LANE]
        k_both = jnp.concatenate([(kk[:, lanes] * lo_f).astype(_BF16),
                                  (kk[:, lanes] * hi_f).astype(_BF16)], axis=0)
        a_cat = jnp.where(lvl == 0.0, _dot_nt(q_b, k_both).astype(_BF16),
                          jnp.zeros((BLK, 2 * BLK), _BF16))
        yield
        for l in range(1, N_LEVELS + 1):
            e = jnp.exp2(grp(l, lanes) if l <= MXU_LEVELS else _level_decay(g_p, 1 << (l - 1))).astype(_BF16)
            k_cat = k_both * jnp.concatenate([e, e], axis=0)
            a_cat = jnp.where(lvl == float(l), _dot_nt(q_b * e, k_cat).astype(_BF16), a_cat)
            yield
        v_cat = jnp.concatenate([v_b * lo_b, v_b * hi_b], axis=0)
        o_intra = _dot(a_cat, v_cat)
        yield

        st = carry['st'][p]
        qg = q_b * jnp.exp2(g_p).astype(_BF16)
        o_inter = _dot_nt(qg, st.astype(_BF16))
        kdec = kk_b[:, lanes] * jnp.exp2(g_p[BLK - 1:BLK, :] - g_p).astype(_BF16)
        upd = lax.dot_general(v_b, kdec, (((0,), (0,)), ((), ())), preferred_element_type=_F32)
        carry['st'][p] = st * decay_tot[:, lanes] + jnp.where(same_head, upd, 0.0)
        outs.append(o_inter + o_intra)
        yield
    o = jnp.concatenate(outs, axis=1)

    ms = _dot(jnp.concatenate(_split_bf16(o * o, 2), axis=1), avg_ref[...])
    gate = pa_ref[0, rows, 2 * A_WIDTH:3 * A_WIDTH].astype(_F32)
    return o * lax.rsqrt(ms + EPS) * gain_ref[...] * _silu(gate)


def _attn_block(rows, n, pb_ref, bias_ref, sink_ref, carry, pbm_ref):
    kv_cur = pb_ref[0, rows, B_WIDTH:B_WIDTH + 2 * B_KV_WIDTH]

    kv_meta = pbm_ref[0, PAD:BLK, B_WIDTH:B_WIDTH + 2 * B_KV_WIDTH]
    kv_all = jnp.concatenate([carry['kv'], kv_cur, kv_meta], axis=0)
    carry['kv'] = kv_cur
    k_all, v_all = kv_all[:, :B_KV_WIDTH], kv_all[:, B_KV_WIDTH:]
    n_keys = 2 * BLK + N_META

    lane1 = lax.broadcasted_iota(jnp.int32, (1, LANE), 1)
    lo_b = (lane1 < B_HEAD_DIM).astype(_BF16)
    hi_b = (lane1 >= B_HEAD_DIM).astype(_BF16)
    n_col = B_Q_HEADS // B_KV_HEADS
    q_cat = jnp.concatenate([pb_ref[0, rows, c * LANE:(c + 1) * LANE] * (lo_b if j == 0 else hi_b)
                             for j in range(B_KV_HEADS) for c in range(n_col)], axis=0)
    sink = jnp.concatenate([jnp.full((1, BLK), sink_ref[h], _F32) for h in range(B_Q_HEADS)], axis=1)
    yield
    bias = bias_ref[jnp.minimum(n, 2)]
    s = _dot_nt(k_all, q_cat) + jnp.concatenate([bias] * B_Q_HEADS, axis=1)
    yield
    mx = jnp.maximum(jnp.max(s, axis=0, keepdims=True), sink)
    yield
    pr = jnp.exp(s - mx).astype(_BF16)
    yield
    num = lax.dot_general(v_all, pr, (((0,), (0,)), ((), ())), preferred_element_type=_F32)
    den = _dot(jnp.ones((16, n_keys), _BF16), pr)[0:1, :] + jnp.exp(sink - mx)
    yield
    o = num / den
    yield
    half = B_HEAD_DIM
    outs = [jnp.concatenate([o[:half, c * BLK:(c + 1) * BLK],
                             o[half:, (n_col + c) * BLK:(n_col + c + 1) * BLK]], axis=0).T
            for c in range(n_col)]
    return jnp.concatenate(outs, axis=1)


def _conv_block(rows, pc_ref, dww_ref, dwb_ref, lng_ref, lnb_ref, tap_ref, carry):
    u = pc_ref[0, rows, :].astype(_F32)
    glu = u[:, :C_CHANNELS] * _sigmoid(u[:, C_CHANNELS:])
    hist = jnp.concatenate([carry['tail'], glu], axis=0)
    carry['tail'] = glu[BLK - HALO:, :]
    yield
    hist_b = hist.astype(_BF16)
    parts = []
    for j in range(C_CONV_WIDTH):
        parts.append(hist_b * dww_ref[j])
        if j % 4 == 3:
            yield
    acc = _dot(tap_ref[...], jnp.concatenate(parts, axis=0))
    yield
    hcv = acc + dwb_ref[...]
    mu = jnp.mean(hcv, axis=-1, keepdims=True)
    ctr = hcv - mu
    yield
    var = jnp.mean(ctr * ctr, axis=-1, keepdims=True)
    hn = ctr * lax.rsqrt(var + EPS) * lng_ref[...] + lnb_ref[...]
    return _silu(hn)


def _interleave(*gens):
    results = [None] * len(gens)
    active = list(enumerate(gens))
    while active:
        still = []
        for k, g in active:
            try:
                next(g)
                still.append((k, g))
            except StopIteration as stop:
                results[k] = stop.value
        active = still
        yield
    return results


def _drain(gen):
    try:
        while True:
            next(gen)
    except StopIteration as stop:
        return stop.value


def _mixer_tile(i, pf_ref, pa_ref, pb_ref, pbp_ref, pbm_ref, pc_ref, pcp_ref, lbc_ref, gain_ref, sink_ref,
                dww_ref, dwb_ref, lng_ref, lnb_ref, cdec_ref, lvl_ref, avg_ref, bias_ref, tap_ref,
                y_ref, st_ref, r_ref):
    u_prev = pcp_ref[0, BLK - HALO:BLK, :].astype(_F32)
    tail = u_prev[:, :C_CHANNELS] * _sigmoid(u_prev[:, C_CHANNELS:])
    carry = {'st': [st_ref[p] for p in range(A_HEADS // 2)],
             'kv': pbp_ref[0, :, B_WIDTH:B_WIDTH + 2 * B_KV_WIDTH],
             'tail': jnp.where(i > 0, tail, 0.0)}
    gens = []
    for blk in range(TILE_BLOCKS):
        rows = slice(blk * BLK, (blk + 1) * BLK)
        n = i * TILE_BLOCKS + blk
        gens += [_hgrn_block(rows, pf_ref, pa_ref, lbc_ref, gain_ref, cdec_ref, lvl_ref, avg_ref, carry,
                             r_ref.at[blk]),
                 _attn_block(rows, n, pb_ref, bias_ref, sink_ref, carry, pbm_ref),
                 _conv_block(rows, pc_ref, dww_ref, dwb_ref, lng_ref, lnb_ref, tap_ref, carry)]
    res = yield from _interleave(*gens)
    for blk in range(TILE_BLOCKS):
        rows = slice(blk * BLK, (blk + 1) * BLK)
        y = jnp.concatenate(res[3 * blk:3 * blk + 3], axis=1)
        if blk == 0:
            row = lax.broadcasted_iota(jnp.int32, y.shape, 0)
            y = jnp.where((i > 0) | (row >= PAD), y, 0.0)
        y_ref[0, rows, :] = y.astype(_BF16)
    for p in range(A_HEADS // 2):
        st_ref[p] = carry['st'][p]


def _mixer_kernel(pf_ref, pa_ref, pb_ref, pbp_ref, pbm_ref, pc_ref, pcp_ref, lbc_ref, gain_ref, sink_ref,
                  dww_ref, dwb_ref, lng_ref, lnb_ref, cdec_ref, lvl_ref, avg_ref, bias_ref, tap_ref,
                  y_ref, st_ref, r_ref):
    i = pl.program_id(1)

    @pl.when(i == 0)
    def _():
        st_ref[...] = jnp.zeros_like(st_ref)

    _drain(_mixer_tile(i, pf_ref, pa_ref, pb_ref, pbp_ref, pbm_ref, pc_ref, pcp_ref, lbc_ref, gain_ref,
                       sink_ref, dww_ref, dwb_ref, lng_ref, lnb_ref, cdec_ref, lvl_ref, avg_ref, bias_ref,
                       tap_ref, y_ref, st_ref, r_ref))


def _mixer_call(pf, pa, pb, pc, lbc, gain, sinks, dww, dwb, lng, lnb):
    bsz, lp, _ = pf.shape
    tile = TILE_BLOCKS * BLK
    assert lp % tile == 0
    cur = lambda w: pl.BlockSpec((1, tile, w), lambda b, i: (b, i, 0))
    prev = lambda w: pl.BlockSpec((1, BLK, w), lambda b, i: (b, jnp.maximum(i * TILE_BLOCKS - 1, 0), 0))
    first = lambda w: pl.BlockSpec((1, BLK, w), lambda b, i: (b, 0, 0))
    const = lambda *shape: pl.BlockSpec(shape, lambda b, i: (0,) * len(shape))
    cdec = jnp.asarray(_decay_matrix(), _BF16)
    lvl = jnp.asarray(_level_matrix(), _BF16)
    avg = jnp.asarray(_head_mean_matrix(), _BF16)
    bias = jnp.asarray(_attn_bias())
    tap = jnp.asarray(_conv_tap_matrix(), _BF16)
    dww = jnp.broadcast_to(dww.astype(_BF16)[:, None, :], (C_CONV_WIDTH, HIST_ROWS, C_CHANNELS))
    return pl.pallas_call(
        _mixer_kernel,
        grid=(bsz, lp // tile),
        in_specs=[cur(A_WIDTH), cur(PA_WIDTH), cur(PB_WIDTH), prev(PB_WIDTH), first(PB_WIDTH),
                  cur(PC_WIDTH), prev(PC_WIDTH),
                  const(3, A_WIDTH), const(1, A_WIDTH), pl.BlockSpec(memory_space=pltpu.SMEM),
                  const(C_CONV_WIDTH, HIST_ROWS, C_CHANNELS), const(1, C_CHANNELS), const(1, C_CHANNELS),
                  const(1, C_CHANNELS),
                  const(*cdec.shape), const(*lvl.shape), const(*avg.shape), const(*bias.shape),
                  const(*tap.shape)],
        out_specs=pl.BlockSpec((1, tile, D_MODEL), lambda b, i: (b, i, 0)),
        out_shape=jax.ShapeDtypeStruct((bsz, lp, D_MODEL), _BF16),
        scratch_shapes=[pltpu.VMEM((A_HEADS // 2, LANE, LANE), _F32),
                        pltpu.VMEM((TILE_BLOCKS, N_GROUPS * BLK, A_WIDTH), _F32)],
        compiler_params=pltpu.CompilerParams(dimension_semantics=("parallel", "arbitrary"),
                                             vmem_limit_bytes=VMEM_LIMIT_BYTES),
        name="mixer",
    )(pf, pa, pb, pb, pb, pc, pc, lbc, gain, sinks, dww, dwb, lng, lnb, cdec, lvl, avg, bias, tap)


def _bq_perm():
    idx = []
    for c in range(B_Q_HEADS // B_KV_HEADS):
        for j in range(B_KV_HEADS):
            head = c + (B_Q_HEADS // B_KV_HEADS) * j
            idx.extend(range(head * B_HEAD_DIM, (head + 1) * B_HEAD_DIM))
    return np.asarray(idx)


def _win_columns():
    q0, f0, i0, g0 = 0, A_WIDTH, 2 * A_WIDTH, 3 * A_WIDTH
    b0 = 4 * A_WIDTH
    bk0 = b0 + B_WIDTH
    c0 = bk0 + 2 * B_KV_WIDTH
    r = lambda s, w: np.arange(s, s + w)
    return np.concatenate([r(f0, A_WIDTH), r(q0, A_WIDTH), r(i0, A_WIDTH), r(g0, A_WIDTH),
                           b0 + _bq_perm(), r(bk0, 2 * B_KV_WIDTH), r(c0, PC_WIDTH)])


def _wout_rows():
    return np.concatenate([np.arange(A_WIDTH), A_WIDTH + _bq_perm(),
                           np.arange(A_WIDTH + B_WIDTH, D_MODEL)])


def kernel(x, meta_tokens, ffn1_norm, ffn1_w_gate, ffn1_w_up, ffn1_w_down, mix_norm, w_in, w_out,
           hgrn_lb_logits, hgrn_out_norm, attn_sinks, conv_dw_w, conv_dw_b, conv_ln_g, conv_ln_b,
           ffn2_norm, ffn2_w_gate, ffn2_w_up, ffn2_w_down, final_norm):
    bsz, seq, _ = x.shape
    depth = w_in.shape[0]
    lp = PAD + N_META + seq
    assert lp % (TILE_BLOCKS * BLK) == 0

    meta = jnp.broadcast_to(meta_tokens[None].astype(x.dtype), (bsz, N_META, D_MODEL))
    h = jnp.concatenate([jnp.zeros((bsz, PAD, D_MODEL), x.dtype), meta, x], axis=1)
    h = h.reshape(bsz * lp, D_MODEL)

    lbs = jnp.cumsum(jax.nn.softmax(hgrn_lb_logits.astype(_F32), axis=0), axis=0)
    lbs = lbs - lbs[0]
    lbc = jnp.stack([jnp.log(lbs), jnp.log1p(-lbs), 1.0 - lbs], axis=1)
    gain = jnp.tile(hgrn_out_norm.astype(_F32), (1, A_HEADS))[:, None, :]
    sinks = attn_sinks.astype(_F32)

    win_cols = _win_columns()
    wout_rows = _wout_rows()
    row2 = lambda a: a.astype(_F32)[None, :]
    qscale = np.ones((IN_WIDTH,), np.float32)
    qscale[A_WIDTH + PA_WIDTH:A_WIDTH + PA_WIDTH + B_WIDTH] = B_HEAD_DIM ** -0.5

    tm = 768 if (bsz * lp) % 768 == 0 else BLK
    for l in range(depth):
        w1 = (ffn1_w_gate[l].astype(_BF16), ffn1_w_up[l].astype(_BF16), ffn1_w_down[l].astype(_BF16))
        w2 = (ffn2_w_gate[l].astype(_BF16), ffn2_w_up[l].astype(_BF16), ffn2_w_down[l].astype(_BF16))
        win = (w_in[l][:, win_cols] * qscale).astype(_BF16)
        wout = w_out[l][wout_rows, :].astype(_BF16)

        h, pf, pa, pb, pc = _ffn_call(h, row2(ffn1_norm[l]), *w1, post=(row2(mix_norm[l]), win), tm=tm)
        shp = lambda a: a.reshape(bsz, lp, a.shape[-1])
        y = _mixer_call(shp(pf), shp(pa), shp(pb), shp(pc), lbc[l], gain[l], sinks[l],
                        conv_dw_w[l].astype(_F32), row2(conv_dw_b[l]), row2(conv_ln_g[l]),
                        row2(conv_ln_b[l]))
        if l == depth - 1:
            blocks = lambda a: a.reshape(bsz, lp // BLK, BLK, D_MODEL)
            return _ffn_call(blocks(h), row2(ffn2_norm[l]), *w2, pre=(blocks(y), wout),
                             final_g=row2(final_norm))
        h = _ffn_call(h, row2(ffn2_norm[l]), *w2, pre=(y.reshape(bsz * lp, D_MODEL), wout), tm=tm)[0]
```

```python
import functools

import numpy as np
import jax
import jax.numpy as jnp
from jax import lax
from jax.experimental import pallas as pl
from jax.experimental.pallas import tpu as pltpu

D_MODEL = 1024
D_FF = 2816
N_META = 16
EPS = 1e-6
LOG2_E = 1.4426950408889634
FFN_HALF = 0.5

A_HEADS = 6
A_DK = 64
A_WIDTH = 384
B_HEAD_DIM = 64
B_WIDTH = 384
B_Q_HEADS = 6
B_KV_HEADS = 2
B_KV_WIDTH = B_KV_HEADS * B_HEAD_DIM
B_WINDOW = 128
C_CHANNELS = 256
C_CONV_WIDTH = 31
IN_WIDTH = 2688

BLK = 128
PAD = BLK - N_META
LANE = 128
HALO = 32

PA_WIDTH = 3 * A_WIDTH
PB_WIDTH = B_WIDTH + 2 * B_KV_WIDTH
PC_WIDTH = 2 * C_CHANNELS

VMEM_LIMIT_BYTES = 56 * 1024 * 1024

_F32 = jnp.float32
_BF16 = jnp.bfloat16


def _dot(a, b):
    return jnp.dot(a, b, preferred_element_type=_F32)


def _dot_nt(a, b):
    return lax.dot_general(a, b, (((1,), (1,)), ((), ())), preferred_element_type=_F32)


def _rms(x, g):
    ms = jnp.mean(x * x, axis=-1, keepdims=True)
    return x * lax.rsqrt(ms + EPS) * g


def _silu(x):
    return x * (1.0 / (1.0 + jnp.exp(-x)))


def _sigmoid(x):
    return 1.0 / (1.0 + jnp.exp(-x))


def _ffn_kernel(*refs, pre, post, final, ff_chunk):
    it = iter(refs)
    h_ref = next(it)
    if pre:
        y_ref, wout_ref = next(it), next(it)
    g_ref, wg_ref, wu_ref, wd_ref = next(it), next(it), next(it), next(it)
    if post:
        g2_ref, win_ref = next(it), next(it)
    if final:
        gf_ref = next(it)
    hout_ref = next(it)
    if post:
        pf_ref, pa_ref, pb_ref, pc_ref = next(it), next(it), next(it), next(it)
    act_ref = next(it)

    rows = hout_ref.shape[-2]
    h = h_ref[...].reshape(rows, D_MODEL)
    if pre:
        h = h + _dot(y_ref[...].reshape(rows, D_MODEL), wout_ref[...])
    xn = _rms(h, g_ref[...]).astype(_BF16)
    for start in range(0, D_FF, ff_chunk):
        cols = slice(start, min(start + ff_chunk, D_FF))
        gate = _dot(xn, wg_ref[:, cols])
        up = _dot(xn, wu_ref[:, cols])
        act_ref[:, cols] = (_silu(gate) * up).astype(_BF16)
    h = h + FFN_HALF * _dot(act_ref[...], wd_ref[...])
    if final:
        hout_ref[...] = _rms(h, gf_ref[...]).reshape(hout_ref.shape)
    else:
        hout_ref[...] = h
    if post:
        xn2 = _rms(h, g2_ref[...]).astype(_BF16)
        pf_ref[...] = _dot(xn2, win_ref[:, 0:A_WIDTH])
        o = A_WIDTH
        pa_ref[...] = _dot(xn2, win_ref[:, o:o + PA_WIDTH]).astype(_BF16)
        o += PA_WIDTH
        pb_ref[...] = _dot(xn2, win_ref[:, o:o + PB_WIDTH]).astype(_BF16)
        o += PB_WIDTH
        pc_ref[...] = _dot(xn2, win_ref[:, o:o + PC_WIDTH]).astype(_BF16)


def _resident(shape):
    return pl.BlockSpec(shape, lambda i: (0,) * len(shape), pipeline_mode=pl.Buffered(1))


def _ffn_call(h, norm_g, wg, wu, wd, *, pre=None, post=None, final_g=None, tm=512, ff_chunk=768):
    if final_g is not None:
        return _final_ffn_call(h, norm_g, wg, wu, wd, pre, final_g, ff_chunk)
    rows = h.shape[0]
    assert rows % tm == 0
    row_spec = lambda w: pl.BlockSpec((tm, w), lambda i: (i, 0))
    args, specs = [h], [row_spec(D_MODEL)]
    if pre is not None:
        y, wout = pre
        args += [y, wout]
        specs += [row_spec(D_MODEL), _resident((D_MODEL, D_MODEL))]
    args += [norm_g, wg, wu, wd]
    specs += [_resident((1, D_MODEL)), _resident((D_MODEL, D_FF)), _resident((D_MODEL, D_FF)),
              _resident((D_FF, D_MODEL))]
    if post is not None:
        g2, win = post
        args += [g2, win]
        specs += [_resident((1, D_MODEL)), _resident((D_MODEL, IN_WIDTH))]
    if final_g is not None:
        args += [final_g]
        specs += [_resident((1, D_MODEL))]
    out_shape = [jax.ShapeDtypeStruct((rows, D_MODEL), _F32)]
    out_specs = [row_spec(D_MODEL)]
    if post is not None:
        for w, dt in ((A_WIDTH, _F32), (PA_WIDTH, _BF16), (PB_WIDTH, _BF16), (PC_WIDTH, _BF16)):
            out_shape.append(jax.ShapeDtypeStruct((rows, w), dt))
            out_specs.append(row_spec(w))
    kern = functools.partial(_ffn_kernel, pre=pre is not None, post=post is not None,
                             final=final_g is not None, ff_chunk=ff_chunk)
    return pl.pallas_call(
        kern,
        grid=(rows // tm,),
        in_specs=specs,
        out_specs=out_specs,
        out_shape=out_shape,
        scratch_shapes=[pltpu.VMEM((tm, D_FF), _BF16)],
        compiler_params=pltpu.CompilerParams(dimension_semantics=("parallel",),
                                             vmem_limit_bytes=VMEM_LIMIT_BYTES),
        name="ffn",
    )(*args)


def _final_ffn_call(h, norm_g, wg, wu, wd, pre, final_g, ff_chunk):
    y, wout = pre
    bsz, nb = h.shape[0], h.shape[1]
    g = max(k for k in (4, 2, 1) if (nb - 1) % k == 0)
    tile = lambda: pl.BlockSpec((pl.Element(1), pl.Element(g), pl.Element(BLK), pl.Element(D_MODEL)),
                                lambda b, j: (b, 1 + g * j, 0, 0))
    res = lambda *shape: pl.BlockSpec(shape, lambda b, j: (0,) * len(shape), pipeline_mode=pl.Buffered(1))
    kern = functools.partial(_ffn_kernel, pre=True, post=False, final=True, ff_chunk=ff_chunk)
    return pl.pallas_call(
        kern,
        grid=(bsz, (nb - 1) // g),
        in_specs=[tile(), tile(), res(D_MODEL, D_MODEL), res(1, D_MODEL), res(D_MODEL, D_FF),
                  res(D_MODEL, D_FF), res(D_FF, D_MODEL), res(1, D_MODEL)],
        out_specs=pl.BlockSpec((1, g * BLK, D_MODEL), lambda b, j: (b, j, 0)),
        out_shape=jax.ShapeDtypeStruct((bsz, (nb - 1) * BLK, D_MODEL), _F32),
        scratch_shapes=[pltpu.VMEM((g * BLK, D_FF), _BF16)],
        compiler_params=pltpu.CompilerParams(dimension_semantics=("parallel", "parallel"),
                                             vmem_limit_bytes=VMEM_LIMIT_BYTES),
        name="ffn_final",
    )(h, y, wout, norm_g, wg, wu, wd, final_g)


N_LEVELS = 7
MXU_LEVELS = 3
N_GROUPS = MXU_LEVELS + 1
DECAY_TERMS = 2
HIST_ROWS = HALO + BLK
CONV_TAP_GROUP = 8
TILE_BLOCKS = 3


def _split_bf16(x, terms):
    out = []
    for _ in range(terms - 1):
        part = x.astype(_BF16)
        out.append(part)
        x = x - part.astype(_F32)
    out.append(x.astype(_BF16))
    return out


def _decay_matrix():
    c = np.zeros((N_GROUPS, BLK, BLK), np.float32)
    for t in range(BLK):
        c[0, t, :t + 1] = 1.0
        for l in range(1, MXU_LEVELS + 1):
            m = 1 << (l - 1)
            ref = (t // (2 * m)) * 2 * m + m - 1
            if t > ref:
                c[l, t, ref + 1:t + 1] = 1.0
            else:
                c[l, t, t + 1:ref + 1] = 1.0
    c = c.reshape(N_GROUPS * BLK, BLK)
    return np.concatenate([c] * DECAY_TERMS, axis=1)


def _level_matrix():
    t = np.arange(BLK)[:, None]
    s = np.arange(BLK)[None, :]
    x = t ^ s
    lvl = np.where(x == 0, 0, np.floor(np.log2(np.maximum(x, 1))) + 1)
    lvl = np.where(s > t, -1, lvl).astype(np.float32)
    return np.concatenate([lvl, lvl], axis=1)


def _head_mean_matrix():
    h = np.arange(A_WIDTH) // A_DK
    return np.where(h[:, None] == h[None, :], 1.0 / A_DK, 0.0).astype(np.float32)


def _conv_tap_matrix():
    first = HALO - (C_CONV_WIDTH - 1)
    m = np.zeros((BLK, C_CONV_WIDTH, HIST_ROWS), np.float32)
    for t in range(BLK):
        for j in range(C_CONV_WIDTH):
            m[t, j, t + first + j] = 1.0
    return m.reshape(BLK, C_CONV_WIDTH * HIST_ROWS)


def _attn_bias():
    n_keys = 2 * BLK + N_META
    qi = np.arange(BLK)[:, None]
    kj = np.arange(n_keys)[None, :]
    out = []
    for n in range(3):
        prev_ok = (kj < BLK) & (kj > qi) & (n >= 2)
        cur_ok = (kj >= BLK) & (kj < 2 * BLK) & (kj - BLK <= qi) & (n >= 1)
        meta_ok = (kj >= 2 * BLK) & ((n >= 1) | (kj - 2 * BLK + PAD <= qi))
        out.append(np.where(prev_ok | cur_ok | meta_ok, 0.0, -1e30).T)
    return np.stack(out).astype(np.float32)


def _level_decay(g_p, m):
    ref = jnp.concatenate([jnp.broadcast_to(g_p[r:r + 1, :], (2 * m, g_p.shape[1]))
                           for r in range(m - 1, BLK, 2 * m)], axis=0)
    return -jnp.abs(g_p - ref)


def _hgrn_block(rows, pf_ref, pa_ref, lbc_ref, gain_ref, cdec_ref, lvl_ref, avg_ref, carry, r_ref):
    row = lax.broadcasted_iota(jnp.int32, (BLK, LANE), 0)
    col = lax.broadcasted_iota(jnp.int32, (BLK, LANE), 1)
    same_head = (row < A_DK) == (col < A_DK)
    lane1 = lax.broadcasted_iota(jnp.int32, (1, LANE), 1)
    lo_f = (lane1 < A_DK).astype(_F32)
    hi_f = 1.0 - lo_f
    lo_b, hi_b = lo_f.astype(_BF16), hi_f.astype(_BF16)

    z = pf_ref[0, rows, :]
    log_lb, log_1m_lb, one_m_lb = lbc_ref[0:1, :], lbc_ref[1:2, :], lbc_ref[2:3, :]
    soft = jnp.log(1.0 + jnp.exp(-jnp.abs(z)))
    log_sig = jnp.minimum(z, 0.0) - soft
    log_sig_neg = jnp.minimum(-z, 0.0) - soft
    b = log_1m_lb + log_sig
    log_f = jnp.maximum(log_lb, b) + jnp.log(1.0 + jnp.exp(-jnp.abs(log_lb - b)))
    kk = one_m_lb * jnp.exp(log_sig_neg)
    kk_b = kk.astype(_BF16)
    yield

    log_f_terms = jnp.concatenate(_split_bf16(log_f * LOG2_E, DECAY_TERMS), axis=0)
    half_rows = N_GROUPS * BLK // 2
    r_ref[0:half_rows, :] = _dot(cdec_ref[0:half_rows, :], log_f_terms)
    r_ref[half_rows:, :] = _dot(cdec_ref[half_rows:, :], log_f_terms)
    grp = lambda g, lanes: r_ref[g * BLK:(g + 1) * BLK, lanes]
    decay_tot = jnp.exp2(r_ref[BLK - 1:BLK, :])
    lvl = lvl_ref[...]
    yield

    def pair(p):
        lanes = slice(p * LANE, (p + 1) * LANE)
        q_b = pa_ref[0, rows, lanes]
        g_p = grp(0, lanes)
        v_b = pa_ref[0, rows, A_WIDTH + p * LANE:A_WIDTH + (p + 1) * LANE]
        k_both = jnp.concatenate([(kk[:, lanes] * lo_f).astype(_BF16),
                                  (kk[:, lanes] * hi_f).astype(_BF16)], axis=0)
        a_cat = jnp.where(lvl == 0.0, _dot_nt(q_b, k_both).astype(_BF16),
                          jnp.zeros((BLK, 2 * BLK), _BF16))
        yield
        for l in range(1, N_LEVELS + 1):
            e = jnp.exp2(grp(l, lanes) if l <= MXU_LEVELS else _level_decay(g_p, 1 << (l - 1))).astype(_BF16)
            k_cat = k_both * jnp.concatenate([e, e], axis=0)
            a_cat = jnp.where(lvl == float(l), _dot_nt(q_b * e, k_cat).astype(_BF16), a_cat)
            yield
        v_cat = jnp.concatenate([v_b * lo_b, v_b * hi_b], axis=0)
        o_intra = _dot(a_cat, v_cat)
        yield

        st = carry['st'][p]
        qg = q_b * jnp.exp2(g_p).astype(_BF16)
        o_inter = _dot_nt(qg, st.astype(_BF16))
        kdec = kk_b[:, lanes] * jnp.exp2(g_p[BLK - 1:BLK, :] - g_p).astype(_BF16)
        upd = lax.dot_general(v_b, kdec, (((0,), (0,)), ((), ())), preferred_element_type=_F32)
        carry['st'][p] = st * decay_tot[:, lanes] + jnp.where(same_head, upd, 0.0)
        return o_inter + o_intra

    outs = yield from _interleave(*[pair(p) for p in range(A_HEADS // 2)])
    o = jnp.concatenate(outs, axis=1)

    ms = _dot((o * o).astype(_BF16), avg_ref[...])
    gate = pa_ref[0, rows, 2 * A_WIDTH:3 * A_WIDTH].astype(_F32)
    return o * lax.rsqrt(ms + EPS) * gain_ref[...] * _silu(gate)


def _attn_block(rows, n, pb_ref, bias_ref, sink_ref, carry, pbm_ref):
    kv_cur = pb_ref[0, rows, B_WIDTH:B_WIDTH + 2 * B_KV_WIDTH]

    kv_meta = pbm_ref[0, PAD:BLK, B_WIDTH:B_WIDTH + 2 * B_KV_WIDTH]
    kv_all = jnp.concatenate([carry['kv'], kv_cur, kv_meta], axis=0)
    carry['kv'] = kv_cur
    k_all, v_all = kv_all[:, :B_KV_WIDTH], kv_all[:, B_KV_WIDTH:]
    n_keys = 2 * BLK + N_META

    lane1 = lax.broadcasted_iota(jnp.int32, (1, LANE), 1)
    lo_b = (lane1 < B_HEAD_DIM).astype(_BF16)
    hi_b = (lane1 >= B_HEAD_DIM).astype(_BF16)
    n_col = B_Q_HEADS // B_KV_HEADS
    q_cat = jnp.concatenate([pb_ref[0, rows, c * LANE:(c + 1) * LANE] * (lo_b if j == 0 else hi_b)
                             for j in range(B_KV_HEADS) for c in range(n_col)], axis=0)
    sink = jnp.concatenate([jnp.full((1, BLK), sink_ref[h], _F32) for h in range(B_Q_HEADS)], axis=1)
    yield
    bias = bias_ref[jnp.minimum(n, 2)]
    s = _dot_nt(k_all, q_cat) + jnp.concatenate([bias] * B_Q_HEADS, axis=1)
    yield
    mx = jnp.maximum(jnp.max(s, axis=0, keepdims=True), sink)
    yield
    pr = jnp.exp(s - mx).astype(_BF16)
    yield
    num = lax.dot_general(v_all, pr, (((0,), (0,)), ((), ())), preferred_element_type=_F32)
    den = _dot(jnp.ones((16, n_keys), _BF16), pr)[0:1, :] + jnp.exp(sink - mx)
    yield
    o = num / den
    yield
    half = B_HEAD_DIM
    outs = [jnp.concatenate([o[:half, c * BLK:(c + 1) * BLK],
                             o[half:, (n_col + c) * BLK:(n_col + c + 1) * BLK]], axis=0).T
            for c in range(n_col)]
    return jnp.concatenate(outs, axis=1)


def _conv_block(rows, pc_ref, dww_ref, dwb_ref, lng_ref, lnb_ref, tap_ref, carry):
    u = pc_ref[0, rows, :].astype(_F32)
    glu = u[:, :C_CHANNELS] * _sigmoid(u[:, C_CHANNELS:])
    hist = jnp.concatenate([carry['tail'], glu], axis=0)
    carry['tail'] = glu[BLK - HALO:, :]
    yield
    hist_b = hist.astype(_BF16)
    acc = None
    for j0 in range(0, C_CONV_WIDTH, CONV_TAP_GROUP):
        taps = range(j0, min(j0 + CONV_TAP_GROUP, C_CONV_WIDTH))
        parts = jnp.concatenate([hist_b * dww_ref[j] for j in taps], axis=0)
        part_sum = _dot(tap_ref[:, j0 * HIST_ROWS:(taps[-1] + 1) * HIST_ROWS], parts)
        acc = part_sum if acc is None else acc + part_sum
        yield
    hcv = acc + dwb_ref[...]
    mu = jnp.mean(hcv, axis=-1, keepdims=True)
    ctr = hcv - mu
    yield
    var = jnp.mean(ctr * ctr, axis=-1, keepdims=True)
    hn = ctr * lax.rsqrt(var + EPS) * lng_ref[...] + lnb_ref[...]
    return _silu(hn)


def _interleave(*gens):
    results = [None] * len(gens)
    active = list(enumerate(gens))
    while active:
        still = []
        for k, g in active:
            try:
                next(g)
                still.append((k, g))
            except StopIteration as stop:
                results[k] = stop.value
        active = still
        yield
    return results


def _drain(gen):
    try:
        while True:
            next(gen)
    except StopIteration as stop:
        return stop.value


def _mixer_tile(i, pf_ref, pa_ref, pb_ref, pbp_ref, pbm_ref, pc_ref, pcp_ref, lbc_ref, gain_ref, sink_ref,
                dww_ref, dwb_ref, lng_ref, lnb_ref, cdec_ref, lvl_ref, avg_ref, bias_ref, tap_ref,
                y_ref, st_ref, r_ref):
    u_prev = pcp_ref[0, BLK - HALO:BLK, :].astype(_F32)
    tail = u_prev[:, :C_CHANNELS] * _sigmoid(u_prev[:, C_CHANNELS:])
    carry = {'st': [st_ref[p] for p in range(A_HEADS // 2)],
             'kv': pbp_ref[0, :, B_WIDTH:B_WIDTH + 2 * B_KV_WIDTH],
             'tail': jnp.where(i > 0, tail, 0.0)}
    gens = []
    for blk in range(TILE_BLOCKS):
        rows = slice(blk * BLK, (blk + 1) * BLK)
        n = i * TILE_BLOCKS + blk
        gens += [_hgrn_block(rows, pf_ref, pa_ref, lbc_ref, gain_ref, cdec_ref, lvl_ref, avg_ref, carry,
                             r_ref.at[blk]),
                 _attn_block(rows, n, pb_ref, bias_ref, sink_ref, carry, pbm_ref),
                 _conv_block(rows, pc_ref, dww_ref, dwb_ref, lng_ref, lnb_ref, tap_ref, carry)]
    res = yield from _interleave(*gens)
    for blk in range(TILE_BLOCKS):
        rows = slice(blk * BLK, (blk + 1) * BLK)
        y = jnp.concatenate(res[3 * blk:3 * blk + 3], axis=1)
        if blk == 0:
            row = lax.broadcasted_iota(jnp.int32, y.shape, 0)
            y = jnp.where((i > 0) | (row >= PAD), y, 0.0)
        y_ref[0, rows, :] = y.astype(_BF16)
    for p in range(A_HEADS // 2):
        st_ref[p] = carry['st'][p]


def _mixer_kernel(pf_ref, pa_ref, pb_ref, pbp_ref, pbm_ref, pc_ref, pcp_ref, lbc_ref, gain_ref, sink_ref,
                  dww_ref, dwb_ref, lng_ref, lnb_ref, cdec_ref, lvl_ref, avg_ref, bias_ref, tap_ref,
                  y_ref, st_ref, r_ref):
    i = pl.program_id(1)

    @pl.when(i == 0)
    def _():
        st_ref[...] = jnp.zeros_like(st_ref)

    _drain(_mixer_tile(i, pf_ref, pa_ref, pb_ref, pbp_ref, pbm_ref, pc_ref, pcp_ref, lbc_ref, gain_ref,
                       sink_ref, dww_ref, dwb_ref, lng_ref, lnb_ref, cdec_ref, lvl_ref, avg_ref, bias_ref,
                       tap_ref, y_ref, st_ref, r_ref))


def _mixer_call(pf, pa, pb, pc, lbc, gain, sinks, dww, dwb, lng, lnb):
    bsz, lp, _ = pf.shape
    tile = TILE_BLOCKS * BLK
    assert lp % tile == 0
    cur = lambda w: pl.BlockSpec((1, tile, w), lambda b, i: (b, i, 0))
    prev = lambda w: pl.BlockSpec((1, BLK, w), lambda b, i: (b, jnp.maximum(i * TILE_BLOCKS - 1, 0), 0))
    first = lambda w: pl.BlockSpec((1, BLK, w), lambda b, i: (b, 0, 0))
    const = lambda *shape: pl.BlockSpec(shape, lambda b, i: (0,) * len(shape))
    cdec = jnp.asarray(_decay_matrix(), _BF16)
    lvl = jnp.asarray(_level_matrix(), _BF16)
    avg = jnp.asarray(_head_mean_matrix(), _BF16)
    bias = jnp.asarray(_attn_bias())
    tap = jnp.asarray(_conv_tap_matrix(), _BF16)
    dww = jnp.broadcast_to(dww.astype(_BF16)[:, None, :], (C_CONV_WIDTH, HIST_ROWS, C_CHANNELS))
    return pl.pallas_call(
        _mixer_kernel,
        grid=(bsz, lp // tile),
        in_specs=[cur(A_WIDTH), cur(PA_WIDTH), cur(PB_WIDTH), prev(PB_WIDTH), first(PB_WIDTH),
                  cur(PC_WIDTH), prev(PC_WIDTH),
                  const(3, A_WIDTH), const(1, A_WIDTH), pl.BlockSpec(memory_space=pltpu.SMEM),
                  const(C_CONV_WIDTH, HIST_ROWS, C_CHANNELS), const(1, C_CHANNELS), const(1, C_CHANNELS),
                  const(1, C_CHANNELS),
                  const(*cdec.shape), const(*lvl.shape), const(*avg.shape), const(*bias.shape),
                  const(*tap.shape)],
        out_specs=pl.BlockSpec((1, tile, D_MODEL), lambda b, i: (b, i, 0)),
        out_shape=jax.ShapeDtypeStruct((bsz, lp, D_MODEL), _BF16),
        scratch_shapes=[pltpu.VMEM((A_HEADS // 2, LANE, LANE), _F32),
                        pltpu.VMEM((TILE_BLOCKS, N_GROUPS * BLK, A_WIDTH), _F32)],
        compiler_params=pltpu.CompilerParams(dimension_semantics=("parallel", "arbitrary"),
                                             vmem_limit_bytes=VMEM_LIMIT_BYTES),
        name="mixer",
    )(pf, pa, pb, pb, pb, pc, pc, lbc, gain, sinks, dww, dwb, lng, lnb, cdec, lvl, avg, bias, tap)


def _bq_ranges(base):
    n_col = B_Q_HEADS // B_KV_HEADS
    heads = [c + n_col * j for c in range(n_col) for j in range(B_KV_HEADS)]
    return [(base + h * B_HEAD_DIM, base + (h + 1) * B_HEAD_DIM) for h in heads]


def _win_ranges():
    b0 = 4 * A_WIDTH
    bk0 = b0 + B_WIDTH
    return ([(A_WIDTH, 2 * A_WIDTH), (0, A_WIDTH), (2 * A_WIDTH, 4 * A_WIDTH)] + _bq_ranges(b0)
            + [(bk0, IN_WIDTH)])


def _wout_ranges():
    return [(0, A_WIDTH)] + _bq_ranges(A_WIDTH) + [(A_WIDTH + B_WIDTH, D_MODEL)]


def _take_ranges(a, ranges, axis):
    return jnp.concatenate([lax.slice_in_dim(a, s, e, axis=axis) for s, e in ranges], axis=axis)


def kernel(x, meta_tokens, ffn1_norm, ffn1_w_gate, ffn1_w_up, ffn1_w_down, mix_norm, w_in, w_out,
           hgrn_lb_logits, hgrn_out_norm, attn_sinks, conv_dw_w, conv_dw_b, conv_ln_g, conv_ln_b,
           ffn2_norm, ffn2_w_gate, ffn2_w_up, ffn2_w_down, final_norm):
    bsz, seq, _ = x.shape
    depth = w_in.shape[0]
    lp = PAD + N_META + seq
    assert lp % (TILE_BLOCKS * BLK) == 0

    meta = jnp.broadcast_to(meta_tokens[None].astype(x.dtype), (bsz, N_META, D_MODEL))
    h = jnp.concatenate([jnp.zeros((bsz, PAD, D_MODEL), x.dtype), meta, x], axis=1)
    h = h.reshape(bsz * lp, D_MODEL)

    lbs = jnp.cumsum(jax.nn.softmax(hgrn_lb_logits.astype(_F32), axis=0), axis=0)
    lbs = lbs - lbs[0]
    lbc = jnp.stack([jnp.log(lbs), jnp.log1p(-lbs), 1.0 - lbs], axis=1)
    gain = jnp.tile(hgrn_out_norm.astype(_F32), (1, A_HEADS))[:, None, :]
    sinks = attn_sinks.astype(_F32)

    row2 = lambda a: a.astype(_F32)[None, :]
    qscale = np.ones((IN_WIDTH,), np.float32)
    qscale[A_WIDTH + PA_WIDTH:A_WIDTH + PA_WIDTH + B_WIDTH] = B_HEAD_DIM ** -0.5
    win_all = (_take_ranges(w_in, _win_ranges(), 2) * qscale).astype(_BF16)
    wout_all = _take_ranges(w_out, _wout_ranges(), 1).astype(_BF16)

    tm = 768 if (bsz * lp) % 768 == 0 else BLK
    for l in range(depth):
        w1 = (ffn1_w_gate[l].astype(_BF16), ffn1_w_up[l].astype(_BF16), ffn1_w_down[l].astype(_BF16))
        w2 = (ffn2_w_gate[l].astype(_BF16), ffn2_w_up[l].astype(_BF16), ffn2_w_down[l].astype(_BF16))
        win, wout = win_all[l], wout_all[l]

        h, pf, pa, pb, pc = _ffn_call(h, row2(ffn1_norm[l]), *w1, post=(row2(mix_norm[l]), win), tm=tm)
        shp = lambda a: a.reshape(bsz, lp, a.shape[-1])
        y = _mixer_call(shp(pf), shp(pa), shp(pb), shp(pc), lbc[l], gain[l], sinks[l],
                        conv_dw_w[l].astype(_F32), row2(conv_dw_b[l]), row2(conv_ln_g[l]),
                        row2(conv_ln_b[l]))
        if l == depth - 1:
            blocks = lambda a: a.reshape(bsz, lp // BLK, BLK, D_MODEL)
            return _ffn_call(blocks(h), row2(ffn2_norm[l]), *w2, pre=(blocks(y), wout),
                             final_g=row2(final_norm))
        h = _ffn_call(h, row2(ffn2_norm[l]), *w2, pre=(y.reshape(bsz * lp, D_MODEL), wout), tm=tm)[0]
```

```python
import functools

import numpy as np
import jax
import jax.numpy as jnp
from jax import lax
from jax.experimental import pallas as pl
from jax.experimental.pallas import tpu as pltpu

D_MODEL = 1024
D_FF = 2816
N_META = 16
EPS = 1e-6
LOG2_E = 1.4426950408889634
FFN_HALF = 0.5

A_HEADS = 6
A_DK = 64
A_WIDTH = 384
B_HEAD_DIM = 64
B_WIDTH = 384
B_Q_HEADS = 6
B_KV_HEADS = 2
B_KV_WIDTH = B_KV_HEADS * B_HEAD_DIM
B_WINDOW = 128
C_CHANNELS = 256
C_CONV_WIDTH = 31
IN_WIDTH = 2688

BLK = 128
PAD = BLK - N_META
LANE = 128
HALO = 32

PA_WIDTH = 3 * A_WIDTH
PB_WIDTH = B_WIDTH + 2 * B_KV_WIDTH
PC_WIDTH = 2 * C_CHANNELS

VMEM_LIMIT_BYTES = 56 * 1024 * 1024

_F32 = jnp.float32
_BF16 = jnp.bfloat16


def _dot(a, b):
    return jnp.dot(a, b, preferred_element_type=_F32)


def _dot_nt(a, b):
    return lax.dot_general(a, b, (((1,), (1,)), ((), ())), preferred_element_type=_F32)


def _rms(x, g):
    ms = jnp.mean(x * x, axis=-1, keepdims=True)
    return x * lax.rsqrt(ms + EPS) * g


def _silu(x):
    return x * (1.0 / (1.0 + jnp.exp(-x)))


def _sigmoid(x):
    return 1.0 / (1.0 + jnp.exp(-x))


def _ffn_kernel(*refs, pre, post, final, ff_chunk):
    it = iter(refs)
    h_ref = next(it)
    if pre:
        y_ref, wout_ref = next(it), next(it)
    g_ref, wg_ref, wu_ref, wd_ref = next(it), next(it), next(it), next(it)
    if post:
        g2_ref, win_ref = next(it), next(it)
    if final:
        gf_ref = next(it)
    hout_ref = next(it)
    if post:
        pf_ref, pa_ref, pb_ref, pc_ref = next(it), next(it), next(it), next(it)
    act_ref = next(it)

    rows = hout_ref.shape[-2]
    h = h_ref[...].reshape(rows, D_MODEL)
    if pre:
        h = h + _dot(y_ref[...].reshape(rows, D_MODEL), wout_ref[...])
    xn = _rms(h, g_ref[...]).astype(_BF16)
    for start in range(0, D_FF, ff_chunk):
        cols = slice(start, min(start + ff_chunk, D_FF))
        gate = _dot(xn, wg_ref[:, cols])
        up = _dot(xn, wu_ref[:, cols])
        act_ref[:, cols] = (_silu(gate) * up).astype(_BF16)
    h = h + FFN_HALF * _dot(act_ref[...], wd_ref[...])
    if final:
        hout_ref[...] = _rms(h, gf_ref[...]).reshape(hout_ref.shape)
    else:
        hout_ref[...] = h
    if post:
        xn2 = _rms(h, g2_ref[...]).astype(_BF16)
        pf_ref[...] = _dot(xn2, win_ref[:, 0:A_WIDTH])
        o = A_WIDTH
        pa_ref[...] = _dot(xn2, win_ref[:, o:o + PA_WIDTH]).astype(_BF16)
        o += PA_WIDTH
        pb_ref[...] = _dot(xn2, win_ref[:, o:o + PB_WIDTH]).astype(_BF16)
        o += PB_WIDTH
        pc_ref[...] = _dot(xn2, win_ref[:, o:o + PC_WIDTH]).astype(_BF16)


def _resident(shape):
    return pl.BlockSpec(shape, lambda *_: (0,) * len(shape), pipeline_mode=pl.Buffered(1))


def _layer_weight(layer, shape):
    return pl.BlockSpec((None,) + shape, lambda *_: (layer,) + (0,) * len(shape),
                        pipeline_mode=pl.Buffered(1))


def _ffn_call(h, layer, norm_g, wg, wu, wd, *, pre=None, post=None, tm=512, ff_chunk=768):
    rows = h.shape[0]
    assert rows % tm == 0
    row_spec = lambda w: pl.BlockSpec((tm, w), lambda i: (i, 0))
    weight = lambda *shape: _layer_weight(layer, shape)
    args, specs = [h], [row_spec(D_MODEL)]
    if pre is not None:
        y, wout = pre
        args += [y, wout]
        specs += [row_spec(D_MODEL), weight(D_MODEL, D_MODEL)]
    args += [norm_g, wg, wu, wd]
    specs += [_resident((1, D_MODEL)), weight(D_MODEL, D_FF), weight(D_MODEL, D_FF), weight(D_FF, D_MODEL)]
    if post is not None:
        g2, win = post
        args += [g2, win]
        specs += [_resident((1, D_MODEL)), weight(D_MODEL, IN_WIDTH)]
    out_shape = [jax.ShapeDtypeStruct((rows, D_MODEL), _F32)]
    out_specs = [row_spec(D_MODEL)]
    if post is not None:
        for w, dt in ((A_WIDTH, _F32), (PA_WIDTH, _BF16), (PB_WIDTH, _BF16), (PC_WIDTH, _BF16)):
            out_shape.append(jax.ShapeDtypeStruct((rows, w), dt))
            out_specs.append(row_spec(w))
    kern = functools.partial(_ffn_kernel, pre=pre is not None, post=post is not None, final=False,
                             ff_chunk=ff_chunk)
    return pl.pallas_call(
        kern,
        grid=(rows // tm,),
        in_specs=specs,
        out_specs=out_specs,
        out_shape=out_shape,
        scratch_shapes=[pltpu.VMEM((tm, D_FF), _BF16)],
        compiler_params=pltpu.CompilerParams(dimension_semantics=("parallel",),
                                             vmem_limit_bytes=VMEM_LIMIT_BYTES),
        name="ffn",
    )(*args)


def _final_ffn_call(h, layer, norm_g, wg, wu, wd, pre, final_g, ff_chunk=768):
    y, wout = pre
    bsz, nb = h.shape[0], h.shape[1]
    g = max(k for k in (4, 2, 1) if (nb - 1) % k == 0)
    tile = lambda: pl.BlockSpec((pl.Element(1), pl.Element(g), pl.Element(BLK), pl.Element(D_MODEL)),
                                lambda b, j: (b, 1 + g * j, 0, 0))
    weight = lambda *shape: _layer_weight(layer, shape)
    kern = functools.partial(_ffn_kernel, pre=True, post=False, final=True, ff_chunk=ff_chunk)
    return pl.pallas_call(
        kern,
        grid=(bsz, (nb - 1) // g),
        in_specs=[tile(), tile(), weight(D_MODEL, D_MODEL), _resident((1, D_MODEL)), weight(D_MODEL, D_FF),
                  weight(D_MODEL, D_FF), weight(D_FF, D_MODEL), _resident((1, D_MODEL))],
        out_specs=pl.BlockSpec((1, g * BLK, D_MODEL), lambda b, j: (b, j, 0)),
        out_shape=jax.ShapeDtypeStruct((bsz, (nb - 1) * BLK, D_MODEL), _F32),
        scratch_shapes=[pltpu.VMEM((g * BLK, D_FF), _BF16)],
        compiler_params=pltpu.CompilerParams(dimension_semantics=("parallel", "parallel"),
                                             vmem_limit_bytes=VMEM_LIMIT_BYTES),
        name="ffn_final",
    )(h, y, wout, norm_g, wg, wu, wd, final_g)


N_LEVELS = 7
MXU_LEVELS = 3
N_GROUPS = MXU_LEVELS + 1
DECAY_TERMS = 2
HIST_ROWS = HALO + BLK
CONV_TAP_GROUP = 8
TILE_BLOCKS = 3


def _split_bf16(x, terms):
    out = []
    for _ in range(terms - 1):
        part = x.astype(_BF16)
        out.append(part)
        x = x - part.astype(_F32)
    out.append(x.astype(_BF16))
    return out


def _decay_matrix():
    c = np.zeros((N_GROUPS, BLK, BLK), np.float32)
    for t in range(BLK):
        c[0, t, :t + 1] = 1.0
        for l in range(1, MXU_LEVELS + 1):
            m = 1 << (l - 1)
            ref = (t // (2 * m)) * 2 * m + m - 1
            if t > ref:
                c[l, t, ref + 1:t + 1] = 1.0
            else:
                c[l, t, t + 1:ref + 1] = 1.0
    c = c.reshape(N_GROUPS * BLK, BLK)
    return np.concatenate([c] * DECAY_TERMS, axis=1)


def _level_matrix():
    t = np.arange(BLK)[:, None]
    s = np.arange(BLK)[None, :]
    x = t ^ s
    lvl = np.where(x == 0, 0, np.floor(np.log2(np.maximum(x, 1))) + 1)
    lvl = np.where(s > t, -1, lvl).astype(np.float32)
    return np.concatenate([lvl, lvl], axis=1)


def _head_mean_matrix():
    h = np.arange(A_WIDTH) // A_DK
    return np.where(h[:, None] == h[None, :], 1.0 / A_DK, 0.0).astype(np.float32)


def _conv_tap_matrix():
    first = HALO - (C_CONV_WIDTH - 1)
    m = np.zeros((BLK, C_CONV_WIDTH, HIST_ROWS), np.float32)
    for t in range(BLK):
        for j in range(C_CONV_WIDTH):
            m[t, j, t + first + j] = 1.0
    return m.reshape(BLK, C_CONV_WIDTH * HIST_ROWS)


def _attn_bias():
    n_keys = 2 * BLK + N_META
    qi = np.arange(BLK)[:, None]
    kj = np.arange(n_keys)[None, :]
    out = []
    for n in range(3):
        prev_ok = (kj < BLK) & (kj > qi) & (n >= 2)
        cur_ok = (kj >= BLK) & (kj < 2 * BLK) & (kj - BLK <= qi) & (n >= 1)
        meta_ok = (kj >= 2 * BLK) & ((n >= 1) | (kj - 2 * BLK + PAD <= qi))
        out.append(np.where(prev_ok | cur_ok | meta_ok, 0.0, -1e30).T)
    return np.stack(out).astype(np.float32)


def _level_decay(g_p, m):
    ref = jnp.concatenate([jnp.broadcast_to(g_p[r:r + 1, :], (2 * m, g_p.shape[1]))
                           for r in range(m - 1, BLK, 2 * m)], axis=0)
    return -jnp.abs(g_p - ref)


def _hgrn_block(rows, pf_ref, pa_ref, lbc_ref, gain_ref, cdec_ref, lvl_ref, avg_ref, carry, r_ref):
    row = lax.broadcasted_iota(jnp.int32, (BLK, LANE), 0)
    col = lax.broadcasted_iota(jnp.int32, (BLK, LANE), 1)
    same_head = (row < A_DK) == (col < A_DK)
    lane1 = lax.broadcasted_iota(jnp.int32, (1, LANE), 1)
    lo_f = (lane1 < A_DK).astype(_F32)
    hi_f = 1.0 - lo_f
    lo_b, hi_b = lo_f.astype(_BF16), hi_f.astype(_BF16)

    z = pf_ref[0, rows, :]
    log_lb, log_1m_lb, one_m_lb = lbc_ref[0:1, :], lbc_ref[1:2, :], lbc_ref[2:3, :]
    soft = jnp.log(1.0 + jnp.exp(-jnp.abs(z)))
    log_sig = jnp.minimum(z, 0.0) - soft
    log_sig_neg = jnp.minimum(-z, 0.0) - soft
    b = log_1m_lb + log_sig
    log_f = jnp.maximum(log_lb, b) + jnp.log(1.0 + jnp.exp(-jnp.abs(log_lb - b)))
    kk = one_m_lb * jnp.exp(log_sig_neg)
    kk_b = kk.astype(_BF16)
    yield

    log_f_terms = jnp.concatenate(_split_bf16(log_f * LOG2_E, DECAY_TERMS), axis=0)
    half_rows = N_GROUPS * BLK // 2
    r_ref[0:half_rows, :] = _dot(cdec_ref[0:half_rows, :], log_f_terms)
    r_ref[half_rows:, :] = _dot(cdec_ref[half_rows:, :], log_f_terms)
    grp = lambda g, lanes: r_ref[g * BLK:(g + 1) * BLK, lanes]
    decay_tot = jnp.exp2(r_ref[BLK - 1:BLK, :])
    lvl = lvl_ref[...]
    yield

    def pair(p):
        lanes = slice(p * LANE, (p + 1) * LANE)
        q_b = pa_ref[0, rows, lanes]
        g_p = grp(0, lanes)
        v_b = pa_ref[0, rows, A_WIDTH + p * LANE:A_WIDTH + (p + 1) * LANE]
        k_both = jnp.concatenate([(kk[:, lanes] * lo_f).astype(_BF16),
                                  (kk[:, lanes] * hi_f).astype(_BF16)], axis=0)
        a_cat = jnp.where(lvl == 0.0, _dot_nt(q_b, k_both).astype(_BF16),
                          jnp.zeros((BLK, 2 * BLK), _BF16))
        yield
        for l in range(1, N_LEVELS + 1):
            e = jnp.exp2(grp(l, lanes) if l <= MXU_LEVELS else _level_decay(g_p, 1 << (l - 1))).astype(_BF16)
            k_cat = k_both * jnp.concatenate([e, e], axis=0)
            a_cat = jnp.where(lvl == float(l), _dot_nt(q_b * e, k_cat).astype(_BF16), a_cat)
            yield
        v_cat = jnp.concatenate([v_b * lo_b, v_b * hi_b], axis=0)
        o_intra = _dot(a_cat, v_cat)
        yield

        st = carry['st'][p]
        qg = q_b * jnp.exp2(g_p).astype(_BF16)
        o_inter = _dot_nt(qg, st.astype(_BF16))
        kdec = kk_b[:, lanes] * jnp.exp2(g_p[BLK - 1:BLK, :] - g_p).astype(_BF16)
        upd = lax.dot_general(v_b, kdec, (((0,), (0,)), ((), ())), preferred_element_type=_F32)
        carry['st'][p] = st * decay_tot[:, lanes] + jnp.where(same_head, upd, 0.0)
        return o_inter + o_intra

    outs = yield from _interleave(*[pair(p) for p in range(A_HEADS // 2)])
    o = jnp.concatenate(outs, axis=1)

    ms = _dot((o * o).astype(_BF16), avg_ref[...])
    gate = pa_ref[0, rows, 2 * A_WIDTH:3 * A_WIDTH].astype(_F32)
    return o * lax.rsqrt(ms + EPS) * gain_ref[...] * _silu(gate)


def _attn_block(rows, n, pb_ref, bias_ref, sink_ref, carry, pbm_ref):
    kv_cur = pb_ref[0, rows, B_WIDTH:B_WIDTH + 2 * B_KV_WIDTH]

    kv_meta = pbm_ref[0, PAD:BLK, B_WIDTH:B_WIDTH + 2 * B_KV_WIDTH]
    kv_all = jnp.concatenate([carry['kv'], kv_cur, kv_meta], axis=0)
    carry['kv'] = kv_cur
    k_all, v_all = kv_all[:, :B_KV_WIDTH], kv_all[:, B_KV_WIDTH:]
    n_keys = 2 * BLK + N_META

    lane1 = lax.broadcasted_iota(jnp.int32, (1, LANE), 1)
    lo_b = (lane1 < B_HEAD_DIM).astype(_BF16)
    hi_b = (lane1 >= B_HEAD_DIM).astype(_BF16)
    n_col = B_Q_HEADS // B_KV_HEADS
    q_cat = jnp.concatenate([pb_ref[0, rows, c * LANE:(c + 1) * LANE] * (lo_b if j == 0 else hi_b)
                             for j in range(B_KV_HEADS) for c in range(n_col)], axis=0)
    sink = jnp.concatenate([jnp.full((1, BLK), sink_ref[h], _F32) for h in range(B_Q_HEADS)], axis=1)
    yield
    bias = bias_ref[jnp.minimum(n, 2)]
    s = _dot_nt(k_all, q_cat) + jnp.concatenate([bias] * B_Q_HEADS, axis=1)
    yield
    mx = jnp.maximum(jnp.max(s, axis=0, keepdims=True), sink)
    yield
    pr = jnp.exp(s - mx).astype(_BF16)
    yield
    num = lax.dot_general(v_all, pr, (((0,), (0,)), ((), ())), preferred_element_type=_F32)
    den = _dot(jnp.ones((16, n_keys), _BF16), pr)[0:1, :] + jnp.exp(sink - mx)
    yield
    o = num / den
    yield
    half = B_HEAD_DIM
    outs = [jnp.concatenate([o[:half, c * BLK:(c + 1) * BLK],
                             o[half:, (n_col + c) * BLK:(n_col + c + 1) * BLK]], axis=0).T
            for c in range(n_col)]
    return jnp.concatenate(outs, axis=1)


def _conv_block(rows, pc_ref, dww_ref, dwb_ref, lng_ref, lnb_ref, tap_ref, carry):
    u = pc_ref[0, rows, :].astype(_F32)
    glu = u[:, :C_CHANNELS] * _sigmoid(u[:, C_CHANNELS:])
    hist = jnp.concatenate([carry['tail'], glu], axis=0)
    carry['tail'] = glu[BLK - HALO:, :]
    yield
    hist_b = hist.astype(_BF16)
    acc = None
    for j0 in range(0, C_CONV_WIDTH, CONV_TAP_GROUP):
        taps = range(j0, min(j0 + CONV_TAP_GROUP, C_CONV_WIDTH))
        parts = jnp.concatenate([hist_b * dww_ref[j] for j in taps], axis=0)
        part_sum = _dot(tap_ref[:, j0 * HIST_ROWS:(taps[-1] + 1) * HIST_ROWS], parts)
        acc = part_sum if acc is None else acc + part_sum
        yield
    hcv = acc + dwb_ref[...]
    mu = jnp.mean(hcv, axis=-1, keepdims=True)
    ctr = hcv - mu
    yield
    var = jnp.mean(ctr * ctr, axis=-1, keepdims=True)
    hn = ctr * lax.rsqrt(var + EPS) * lng_ref[...] + lnb_ref[...]
    return _silu(hn)


def _interleave(*gens):
    results = [None] * len(gens)
    active = list(enumerate(gens))
    while active:
        still = []
        for k, g in active:
            try:
                next(g)
                still.append((k, g))
            except StopIteration as stop:
                results[k] = stop.value
        active = still
        yield
    return results


def _drain(gen):
    try:
        while True:
            next(gen)
    except StopIteration as stop:
        return stop.value


def _mixer_tile(i, pf_ref, pa_ref, pb_ref, pbp_ref, pbm_ref, pc_ref, pcp_ref, lbc_ref, gain_ref, sink_ref,
                dww_ref, dwb_ref, lng_ref, lnb_ref, cdec_ref, lvl_ref, avg_ref, bias_ref, tap_ref,
                y_ref, st_ref, r_ref):
    u_prev = pcp_ref[0, BLK - HALO:BLK, :].astype(_F32)
    tail = u_prev[:, :C_CHANNELS] * _sigmoid(u_prev[:, C_CHANNELS:])
    carry = {'st': [st_ref[p] for p in range(A_HEADS // 2)],
             'kv': pbp_ref[0, :, B_WIDTH:B_WIDTH + 2 * B_KV_WIDTH],
             'tail': jnp.where(i > 0, tail, 0.0)}
    gens = []
    for blk in range(TILE_BLOCKS):
        rows = slice(blk * BLK, (blk + 1) * BLK)
        n = i * TILE_BLOCKS + blk
        gens += [_hgrn_block(rows, pf_ref, pa_ref, lbc_ref, gain_ref, cdec_ref, lvl_ref, avg_ref, carry,
                             r_ref.at[blk]),
                 _attn_block(rows, n, pb_ref, bias_ref, sink_ref, carry, pbm_ref),
                 _conv_block(rows, pc_ref, dww_ref, dwb_ref, lng_ref, lnb_ref, tap_ref, carry)]
    res = yield from _interleave(*gens)
    for blk in range(TILE_BLOCKS):
        rows = slice(blk * BLK, (blk + 1) * BLK)
        y = jnp.concatenate(res[3 * blk:3 * blk + 3], axis=1)
        if blk == 0:
            row = lax.broadcasted_iota(jnp.int32, y.shape, 0)
            y = jnp.where((i > 0) | (row >= PAD), y, 0.0)
        y_ref[0, rows, :] = y.astype(_BF16)
    for p in range(A_HEADS // 2):
        st_ref[p] = carry['st'][p]


def _mixer_kernel(pf_ref, pa_ref, pb_ref, pbp_ref, pbm_ref, pc_ref, pcp_ref, lbc_ref, gain_ref, sink_ref,
                  dww_ref, dwb_ref, lng_ref, lnb_ref, cdec_ref, lvl_ref, avg_ref, bias_ref, tap_ref,
                  y_ref, st_ref, r_ref):
    i = pl.program_id(1)

    @pl.when(i == 0)
    def _():
        st_ref[...] = jnp.zeros_like(st_ref)

    _drain(_mixer_tile(i, pf_ref, pa_ref, pb_ref, pbp_ref, pbm_ref, pc_ref, pcp_ref, lbc_ref, gain_ref,
                       sink_ref, dww_ref, dwb_ref, lng_ref, lnb_ref, cdec_ref, lvl_ref, avg_ref, bias_ref,
                       tap_ref, y_ref, st_ref, r_ref))


def _mixer_call(pf, pa, pb, pc, lbc, gain, sinks, dww, dwb, lng, lnb):
    bsz, lp, _ = pf.shape
    tile = TILE_BLOCKS * BLK
    assert lp % tile == 0
    cur = lambda w: pl.BlockSpec((1, tile, w), lambda b, i: (b, i, 0))
    prev = lambda w: pl.BlockSpec((1, BLK, w), lambda b, i: (b, jnp.maximum(i * TILE_BLOCKS - 1, 0), 0))
    first = lambda w: pl.BlockSpec((1, BLK, w), lambda b, i: (b, 0, 0))
    const = lambda *shape: pl.BlockSpec(shape, lambda b, i: (0,) * len(shape))
    cdec = jnp.asarray(_decay_matrix(), _BF16)
    lvl = jnp.asarray(_level_matrix(), _BF16)
    avg = jnp.asarray(_head_mean_matrix(), _BF16)
    bias = jnp.asarray(_attn_bias())
    tap = jnp.asarray(_conv_tap_matrix(), _BF16)
    dww = jnp.broadcast_to(dww.astype(_BF16)[:, None, :], (C_CONV_WIDTH, HIST_ROWS, C_CHANNELS))
    return pl.pallas_call(
        _mixer_kernel,
        grid=(bsz, lp // tile),
        in_specs=[cur(A_WIDTH), cur(PA_WIDTH), cur(PB_WIDTH), prev(PB_WIDTH), first(PB_WIDTH),
                  cur(PC_WIDTH), prev(PC_WIDTH),
                  const(3, A_WIDTH), const(1, A_WIDTH), pl.BlockSpec(memory_space=pltpu.SMEM),
                  const(C_CONV_WIDTH, HIST_ROWS, C_CHANNELS), const(1, C_CHANNELS), const(1, C_CHANNELS),
                  const(1, C_CHANNELS),
                  const(*cdec.shape), const(*lvl.shape), const(*avg.shape), const(*bias.shape),
                  const(*tap.shape)],
        out_specs=pl.BlockSpec((1, tile, D_MODEL), lambda b, i: (b, i, 0)),
        out_shape=jax.ShapeDtypeStruct((bsz, lp, D_MODEL), _BF16),
        scratch_shapes=[pltpu.VMEM((A_HEADS // 2, LANE, LANE), _F32),
                        pltpu.VMEM((TILE_BLOCKS, N_GROUPS * BLK, A_WIDTH), _F32)],
        compiler_params=pltpu.CompilerParams(dimension_semantics=("parallel", "arbitrary"),
                                             vmem_limit_bytes=VMEM_LIMIT_BYTES),
        name="mixer",
    )(pf, pa, pb, pb, pb, pc, pc, lbc, gain, sinks, dww, dwb, lng, lnb, cdec, lvl, avg, bias, tap)


def _bq_ranges(base):
    n_col = B_Q_HEADS // B_KV_HEADS
    heads = [c + n_col * j for c in range(n_col) for j in range(B_KV_HEADS)]
    return [(base + h * B_HEAD_DIM, base + (h + 1) * B_HEAD_DIM) for h in heads]


def _win_ranges():
    b0 = 4 * A_WIDTH
    bk0 = b0 + B_WIDTH
    return ([(A_WIDTH, 2 * A_WIDTH), (0, A_WIDTH), (2 * A_WIDTH, 4 * A_WIDTH)] + _bq_ranges(b0)
            + [(bk0, IN_WIDTH)])


def _wout_ranges():
    return [(0, A_WIDTH)] + _bq_ranges(A_WIDTH) + [(A_WIDTH + B_WIDTH, D_MODEL)]


def _take_ranges(a, ranges, axis):
    return jnp.concatenate([lax.slice_in_dim(a, s, e, axis=axis) for s, e in ranges], axis=axis)


def kernel(x, meta_tokens, ffn1_norm, ffn1_w_gate, ffn1_w_up, ffn1_w_down, mix_norm, w_in, w_out,
           hgrn_lb_logits, hgrn_out_norm, attn_sinks, conv_dw_w, conv_dw_b, conv_ln_g, conv_ln_b,
           ffn2_norm, ffn2_w_gate, ffn2_w_up, ffn2_w_down, final_norm):
    bsz, seq, _ = x.shape
    depth = w_in.shape[0]
    lp = PAD + N_META + seq
    assert lp % (TILE_BLOCKS * BLK) == 0

    meta = jnp.broadcast_to(meta_tokens[None].astype(x.dtype), (bsz, N_META, D_MODEL))
    h = jnp.concatenate([jnp.zeros((bsz, PAD, D_MODEL), x.dtype), meta, x], axis=1)
    h = h.reshape(bsz * lp, D_MODEL)

    lbs = jnp.cumsum(jax.nn.softmax(hgrn_lb_logits.astype(_F32), axis=0), axis=0)
    lbs = lbs - lbs[0]
    lbc = jnp.stack([jnp.log(lbs), jnp.log1p(-lbs), 1.0 - lbs], axis=1)
    gain = jnp.tile(hgrn_out_norm.astype(_F32), (1, A_HEADS))[:, None, :]
    sinks = attn_sinks.astype(_F32)

    row2 = lambda a: a.astype(_F32)[None, :]
    win_all = jnp.concatenate(
        [(lax.slice_in_dim(w_in, s, e, axis=2) * (B_HEAD_DIM ** -0.5 if (s, e) in _bq_ranges(4 * A_WIDTH) else 1.0)
          ).astype(_BF16) for s, e in _win_ranges()], axis=2)
    wout_all = _take_ranges(w_out.astype(_BF16), _wout_ranges(), 1)
    w1 = (ffn1_w_gate.astype(_BF16), ffn1_w_up.astype(_BF16), ffn1_w_down.astype(_BF16))
    w2 = (ffn2_w_gate.astype(_BF16), ffn2_w_up.astype(_BF16), ffn2_w_down.astype(_BF16))

    tm = 768 if (bsz * lp) % 768 == 0 else BLK
    for l in range(depth):
        h, pf, pa, pb, pc = _ffn_call(h, l, row2(ffn1_norm[l]), *w1, post=(row2(mix_norm[l]), win_all), tm=tm)
        shp = lambda a: a.reshape(bsz, lp, a.shape[-1])
        y = _mixer_call(shp(pf), shp(pa), shp(pb), shp(pc), lbc[l], gain[l], sinks[l],
                        conv_dw_w[l].astype(_F32), row2(conv_dw_b[l]), row2(conv_ln_g[l]),
                        row2(conv_ln_b[l]))
        if l == depth - 1:
            blocks = lambda a: a.reshape(bsz, lp // BLK, BLK, D_MODEL)
            return _final_ffn_call(blocks(h), l, row2(ffn2_norm[l]), *w2, (blocks(y), wout_all),
                                   row2(final_norm))
        h = _ffn_call(h, l, row2(ffn2_norm[l]), *w2, pre=(y.reshape(bsz * lp, D_MODEL), wout_all), tm=tm)[0]
```

```python
import functools

import numpy as np
import jax
import jax.numpy as jnp
from jax import lax
from jax.experimental import pallas as pl
from jax.experimental.pallas import tpu as pltpu

D_MODEL = 1024
D_FF = 2816
N_META = 16
EPS = 1e-6
LOG2_E = 1.4426950408889634
FFN_HALF = 0.5

A_HEADS = 6
A_DK = 64
A_WIDTH = 384
B_HEAD_DIM = 64
B_WIDTH = 384
B_Q_HEADS = 6
B_KV_HEADS = 2
B_KV_WIDTH = B_KV_HEADS * B_HEAD_DIM
B_WINDOW = 128
C_CHANNELS = 256
C_CONV_WIDTH = 31
IN_WIDTH = 2688

BLK = 128
PAD = BLK - N_META
LANE = 128
HALO = 32

PA_WIDTH = 3 * A_WIDTH
PB_WIDTH = B_WIDTH + 2 * B_KV_WIDTH
PC_WIDTH = 2 * C_CHANNELS

VMEM_LIMIT_BYTES = 56 * 1024 * 1024

_F32 = jnp.float32
_BF16 = jnp.bfloat16


def _dot(a, b):
    return jnp.dot(a, b, preferred_element_type=_F32)


def _dot_nt(a, b):
    return lax.dot_general(a, b, (((1,), (1,)), ((), ())), preferred_element_type=_F32)


def _rms(x, g):
    ms = jnp.mean(x * x, axis=-1, keepdims=True)
    return x * lax.rsqrt(ms + EPS) * g


def _silu(x):
    return x * (1.0 / (1.0 + jnp.exp(-x)))


def _sigmoid(x):
    return 1.0 / (1.0 + jnp.exp(-x))


def _ffn_kernel(*refs, pre, post, final, ff_chunk, first=False):
    it = iter(refs)
    h_ref = next(it)
    if first:
        head_ref = next(it)
    if pre:
        y_ref, wout_ref = next(it), next(it)
    g_ref, wg_ref, wu_ref, wd_ref = next(it), next(it), next(it), next(it)
    if post:
        g2_ref, win_ref = next(it), next(it)
    if final:
        gf_ref = next(it)
    hout_ref = next(it)
    if post:
        pf_ref, pa_ref, pb_ref, pc_ref = next(it), next(it), next(it), next(it)
    act_ref = next(it)

    rows = hout_ref.shape[-2]
    h = h_ref[...].reshape(rows, D_MODEL)
    if first:
        shifted = jnp.concatenate([head_ref[...], h[:rows - BLK, :]], axis=0)
        h = jnp.where(pl.program_id(1) == 0, shifted, h)
    if pre:
        h = h + _dot(y_ref[...].reshape(rows, D_MODEL), wout_ref[...])
    xn = _rms(h, g_ref[...]).astype(_BF16)
    for start in range(0, D_FF, ff_chunk):
        cols = slice(start, min(start + ff_chunk, D_FF))
        gate = _dot(xn, wg_ref[:, cols])
        up = _dot(xn, wu_ref[:, cols])
        act_ref[:, cols] = (_silu(gate) * up).astype(_BF16)
    h = h + FFN_HALF * _dot(act_ref[...], wd_ref[...])
    if final:
        hout_ref[...] = _rms(h, gf_ref[...]).reshape(hout_ref.shape)
    else:
        hout_ref[...] = h
    if post:
        xn2 = _rms(h, g2_ref[...]).astype(_BF16)
        pf_ref[...] = _dot(xn2, win_ref[:, 0:A_WIDTH])
        o = A_WIDTH
        pa_ref[...] = _dot(xn2, win_ref[:, o:o + PA_WIDTH]).astype(_BF16)
        o += PA_WIDTH
        pb_ref[...] = _dot(xn2, win_ref[:, o:o + PB_WIDTH]).astype(_BF16)
        o += PB_WIDTH
        pc_ref[...] = _dot(xn2, win_ref[:, o:o + PC_WIDTH]).astype(_BF16)


def _resident(shape):
    return pl.BlockSpec(shape, lambda *_: (0,) * len(shape), pipeline_mode=pl.Buffered(1))


def _layer_weight(layer, shape):
    return pl.BlockSpec((None,) + shape, lambda *_: (layer,) + (0,) * len(shape),
                        pipeline_mode=pl.Buffered(1))


def _ffn_call(h, layer, norm_g, wg, wu, wd, *, pre=None, post=None, tm=512, ff_chunk=768):
    rows = h.shape[0]
    assert rows % tm == 0
    row_spec = lambda w: pl.BlockSpec((tm, w), lambda i: (i, 0))
    weight = lambda *shape: _layer_weight(layer, shape)
    args, specs = [h], [row_spec(D_MODEL)]
    if pre is not None:
        y, wout = pre
        args += [y, wout]
        specs += [row_spec(D_MODEL), weight(D_MODEL, D_MODEL)]
    args += [norm_g, wg, wu, wd]
    specs += [_resident((1, D_MODEL)), weight(D_MODEL, D_FF), weight(D_MODEL, D_FF), weight(D_FF, D_MODEL)]
    if post is not None:
        g2, win = post
        args += [g2, win]
        specs += [_resident((1, D_MODEL)), weight(D_MODEL, IN_WIDTH)]
    out_shape = [jax.ShapeDtypeStruct((rows, D_MODEL), _F32)]
    out_specs = [row_spec(D_MODEL)]
    if post is not None:
        for w, dt in ((A_WIDTH, _F32), (PA_WIDTH, _BF16), (PB_WIDTH, _BF16), (PC_WIDTH, _BF16)):
            out_shape.append(jax.ShapeDtypeStruct((rows, w), dt))
            out_specs.append(row_spec(w))
    kern = functools.partial(_ffn_kernel, pre=pre is not None, post=post is not None, final=False,
                             ff_chunk=ff_chunk)
    return pl.pallas_call(
        kern,
        grid=(rows // tm,),
        in_specs=specs,
        out_specs=out_specs,
        out_shape=out_shape,
        scratch_shapes=[pltpu.VMEM((tm, D_FF), _BF16)],
        compiler_params=pltpu.CompilerParams(dimension_semantics=("parallel",),
                                             vmem_limit_bytes=VMEM_LIMIT_BYTES),
        name="ffn",
    )(*args)


def _first_ffn_call(x, head, layer, norm_g, wg, wu, wd, post, ff_chunk=768):
    bsz, nbx = x.shape[0], x.shape[1]
    g = TILE_BLOCKS
    tiles = (nbx + 1) // g
    assert tiles * g == nbx + 1
    g2, win = post
    weight = lambda *shape: _layer_weight(layer, shape)
    x_tile = pl.BlockSpec((pl.Element(1), pl.Element(g), pl.Element(BLK), pl.Element(D_MODEL)),
                          lambda b, j: (b, jnp.maximum(g * j - 1, 0), 0, 0))
    row_spec = lambda w: pl.BlockSpec((g * BLK, w), lambda b, j: (b * tiles + j, 0))
    widths = ((D_MODEL, _F32), (A_WIDTH, _F32), (PA_WIDTH, _BF16), (PB_WIDTH, _BF16), (PC_WIDTH, _BF16))
    kern = functools.partial(_ffn_kernel, pre=False, post=True, final=False, ff_chunk=ff_chunk, first=True)
    return pl.pallas_call(
        kern,
        grid=(bsz, tiles),
        in_specs=[x_tile, _resident((BLK, D_MODEL)), _resident((1, D_MODEL)), weight(D_MODEL, D_FF),
                  weight(D_MODEL, D_FF), weight(D_FF, D_MODEL), _resident((1, D_MODEL)),
                  weight(D_MODEL, IN_WIDTH)],
        out_specs=[row_spec(w) for w, _ in widths],
        out_shape=[jax.ShapeDtypeStruct((bsz * tiles * g * BLK, w), dt) for w, dt in widths],
        scratch_shapes=[pltpu.VMEM((g * BLK, D_FF), _BF16)],
        compiler_params=pltpu.CompilerParams(dimension_semantics=("parallel", "parallel"),
                                             vmem_limit_bytes=VMEM_LIMIT_BYTES),
        name="ffn_first",
    )(x, head, norm_g, wg, wu, wd, g2, win)


def _final_ffn_call(h, layer, norm_g, wg, wu, wd, pre, final_g, ff_chunk=768):
    y, wout = pre
    bsz, nb = h.shape[0], h.shape[1]
    g = max(k for k in (4, 2, 1) if (nb - 1) % k == 0)
    tile = lambda: pl.BlockSpec((pl.Element(1), pl.Element(g), pl.Element(BLK), pl.Element(D_MODEL)),
                                lambda b, j: (b, 1 + g * j, 0, 0))
    weight = lambda *shape: _layer_weight(layer, shape)
    kern = functools.partial(_ffn_kernel, pre=True, post=False, final=True, ff_chunk=ff_chunk)
    return pl.pallas_call(
        kern,
        grid=(bsz, (nb - 1) // g),
        in_specs=[tile(), tile(), weight(D_MODEL, D_MODEL), _resident((1, D_MODEL)), weight(D_MODEL, D_FF),
                  weight(D_MODEL, D_FF), weight(D_FF, D_MODEL), _resident((1, D_MODEL))],
        out_specs=pl.BlockSpec((1, g * BLK, D_MODEL), lambda b, j: (b, j, 0)),
        out_shape=jax.ShapeDtypeStruct((bsz, (nb - 1) * BLK, D_MODEL), _F32),
        scratch_shapes=[pltpu.VMEM((g * BLK, D_FF), _BF16)],
        compiler_params=pltpu.CompilerParams(dimension_semantics=("parallel", "parallel"),
                                             vmem_limit_bytes=VMEM_LIMIT_BYTES),
        name="ffn_final",
    )(h, y, wout, norm_g, wg, wu, wd, final_g)


N_LEVELS = 7
MXU_LEVELS = 3
N_GROUPS = MXU_LEVELS + 1
DECAY_TERMS = 2
HIST_ROWS = HALO + BLK
CONV_TAP_GROUP = 8
TILE_BLOCKS = 3


def _split_bf16(x, terms):
    out = []
    for _ in range(terms - 1):
        part = x.astype(_BF16)
        out.append(part)
        x = x - part.astype(_F32)
    out.append(x.astype(_BF16))
    return out


def _decay_matrix():
    c = np.zeros((N_GROUPS, BLK, BLK), np.float32)
    for t in range(BLK):
        c[0, t, :t + 1] = 1.0
        for l in range(1, MXU_LEVELS + 1):
            m = 1 << (l - 1)
            ref = (t // (2 * m)) * 2 * m + m - 1
            if t > ref:
                c[l, t, ref + 1:t + 1] = 1.0
            else:
                c[l, t, t + 1:ref + 1] = 1.0
    c = c.reshape(N_GROUPS * BLK, BLK)
    return np.concatenate([c] * DECAY_TERMS, axis=1)


def _level_matrix():
    t = np.arange(BLK)[:, None]
    s = np.arange(BLK)[None, :]
    x = t ^ s
    lvl = np.where(x == 0, 0, np.floor(np.log2(np.maximum(x, 1))) + 1)
    lvl = np.where(s > t, -1, lvl).astype(np.float32)
    return np.concatenate([lvl, lvl], axis=1)


def _head_mean_matrix():
    h = np.arange(A_WIDTH) // A_DK
    return np.where(h[:, None] == h[None, :], 1.0 / A_DK, 0.0).astype(np.float32)


def _conv_tap_matrix():
    first = HALO - (C_CONV_WIDTH - 1)
    m = np.zeros((BLK, C_CONV_WIDTH, HIST_ROWS), np.float32)
    for t in range(BLK):
        for j in range(C_CONV_WIDTH):
            m[t, j, t + first + j] = 1.0
    return m.reshape(BLK, C_CONV_WIDTH * HIST_ROWS)


def _attn_bias():
    n_keys = 2 * BLK + N_META
    qi = np.arange(BLK)[:, None]
    kj = np.arange(n_keys)[None, :]
    out = []
    for n in range(3):
        prev_ok = (kj < BLK) & (kj > qi) & (n >= 2)
        cur_ok = (kj >= BLK) & (kj < 2 * BLK) & (kj - BLK <= qi) & (n >= 1)
        meta_ok = (kj >= 2 * BLK) & ((n >= 1) | (kj - 2 * BLK + PAD <= qi))
        out.append(np.where(prev_ok | cur_ok | meta_ok, 0.0, -1e30).T)
    return np.stack(out).astype(np.float32)


def _level_decay(g_p, m):
    ref = jnp.concatenate([jnp.broadcast_to(g_p[r:r + 1, :], (2 * m, g_p.shape[1]))
                           for r in range(m - 1, BLK, 2 * m)], axis=0)
    return -jnp.abs(g_p - ref)


def _hgrn_block(rows, pf_ref, pa_ref, lbc_ref, gain_ref, cdec_ref, lvl_ref, avg_ref, carry, r_ref):
    row = lax.broadcasted_iota(jnp.int32, (BLK, LANE), 0)
    col = lax.broadcasted_iota(jnp.int32, (BLK, LANE), 1)
    same_head = (row < A_DK) == (col < A_DK)
    lane1 = lax.broadcasted_iota(jnp.int32, (1, LANE), 1)
    lo_f = (lane1 < A_DK).astype(_F32)
    hi_f = 1.0 - lo_f
    lo_b, hi_b = lo_f.astype(_BF16), hi_f.astype(_BF16)

    z = pf_ref[0, rows, :]
    log_lb, log_1m_lb, one_m_lb = lbc_ref[0:1, :], lbc_ref[1:2, :], lbc_ref[2:3, :]
    soft = jnp.log(1.0 + jnp.exp(-jnp.abs(z)))
    log_sig = jnp.minimum(z, 0.0) - soft
    log_sig_neg = jnp.minimum(-z, 0.0) - soft
    b = log_1m_lb + log_sig
    log_f = jnp.maximum(log_lb, b) + jnp.log(1.0 + jnp.exp(-jnp.abs(log_lb - b)))
    kk = one_m_lb * jnp.exp(log_sig_neg)
    kk_b = kk.astype(_BF16)
    yield

    log_f_terms = jnp.concatenate(_split_bf16(log_f * LOG2_E, DECAY_TERMS), axis=0)
    half_rows = N_GROUPS * BLK // 2
    r_ref[0:half_rows, :] = _dot(cdec_ref[0:half_rows, :], log_f_terms)
    r_ref[half_rows:, :] = _dot(cdec_ref[half_rows:, :], log_f_terms)
    grp = lambda g, lanes: r_ref[g * BLK:(g + 1) * BLK, lanes]
    decay_tot = jnp.exp2(r_ref[BLK - 1:BLK, :])
    lvl = lvl_ref[...]
    yield

    def pair(p):
        lanes = slice(p * LANE, (p + 1) * LANE)
        q_b = pa_ref[0, rows, lanes]
        g_p = grp(0, lanes)
        v_b = pa_ref[0, rows, A_WIDTH + p * LANE:A_WIDTH + (p + 1) * LANE]
        k_both = jnp.concatenate([(kk[:, lanes] * lo_f).astype(_BF16),
                                  (kk[:, lanes] * hi_f).astype(_BF16)], axis=0)
        a_cat = jnp.where(lvl == 0.0, _dot_nt(q_b, k_both).astype(_BF16),
                          jnp.zeros((BLK, 2 * BLK), _BF16))
        yield
        for l in range(1, N_LEVELS + 1):
            e = jnp.exp2(grp(l, lanes) if l <= MXU_LEVELS else _level_decay(g_p, 1 << (l - 1))).astype(_BF16)
            k_cat = k_both * jnp.concatenate([e, e], axis=0)
            a_cat = jnp.where(lvl == float(l), _dot_nt(q_b * e, k_cat).astype(_BF16), a_cat)
            yield
        v_cat = jnp.concatenate([v_b * lo_b, v_b * hi_b], axis=0)
        o_intra = _dot(a_cat, v_cat)
        yield

        st = carry['st'][p]
        qg = q_b * jnp.exp2(g_p).astype(_BF16)
        o_inter = _dot_nt(qg, st.astype(_BF16))
        kdec = kk_b[:, lanes] * jnp.exp2(g_p[BLK - 1:BLK, :] - g_p).astype(_BF16)
        upd = lax.dot_general(v_b, kdec, (((0,), (0,)), ((), ())), preferred_element_type=_F32)
        carry['st'][p] = st * decay_tot[:, lanes] + jnp.where(same_head, upd, 0.0)
        return o_inter + o_intra

    outs = yield from _interleave(*[pair(p) for p in range(A_HEADS // 2)])
    o = jnp.concatenate(outs, axis=1)

    ms = _dot((o * o).astype(_BF16), avg_ref[...])
    gate = pa_ref[0, rows, 2 * A_WIDTH:3 * A_WIDTH].astype(_F32)
    return o * lax.rsqrt(ms + EPS) * gain_ref[...] * _silu(gate)


def _attn_block(rows, n, pb_ref, bias_ref, sink_ref, carry, pbm_ref):
    kv_cur = pb_ref[0, rows, B_WIDTH:B_WIDTH + 2 * B_KV_WIDTH]

    kv_meta = pbm_ref[0, PAD:BLK, B_WIDTH:B_WIDTH + 2 * B_KV_WIDTH]
    kv_all = jnp.concatenate([carry['kv'], kv_cur, kv_meta], axis=0)
    carry['kv'] = kv_cur
    k_all, v_all = kv_all[:, :B_KV_WIDTH], kv_all[:, B_KV_WIDTH:]
    n_keys = 2 * BLK + N_META

    lane1 = lax.broadcasted_iota(jnp.int32, (1, LANE), 1)
    lo_b = (lane1 < B_HEAD_DIM).astype(_BF16)
    hi_b = (lane1 >= B_HEAD_DIM).astype(_BF16)
    n_col = B_Q_HEADS // B_KV_HEADS
    q_cat = jnp.concatenate([pb_ref[0, rows, c * LANE:(c + 1) * LANE] * (lo_b if j == 0 else hi_b)
                             for j in range(B_KV_HEADS) for c in range(n_col)], axis=0)
    sink = jnp.concatenate([jnp.full((1, BLK), sink_ref[h], _F32) for h in range(B_Q_HEADS)], axis=1)
    yield
    bias = bias_ref[jnp.minimum(n, 2)]
    s = _dot_nt(k_all, q_cat) + jnp.concatenate([bias] * B_Q_HEADS, axis=1)
    yield
    mx = jnp.maximum(jnp.max(s, axis=0, keepdims=True), sink)
    yield
    pr = jnp.exp(s - mx).astype(_BF16)
    yield
    num = lax.dot_general(v_all, pr, (((0,), (0,)), ((), ())), preferred_element_type=_F32)
    den = _dot(jnp.ones((16, n_keys), _BF16), pr)[0:1, :] + jnp.exp(sink - mx)
    yield
    o = num / den
    yield
    half = B_HEAD_DIM
    outs = [jnp.concatenate([o[:half, c * BLK:(c + 1) * BLK],
                             o[half:, (n_col + c) * BLK:(n_col + c + 1) * BLK]], axis=0).T
            for c in range(n_col)]
    return jnp.concatenate(outs, axis=1)


def _conv_block(rows, pc_ref, dww_ref, dwb_ref, lng_ref, lnb_ref, tap_ref, carry):
    u = pc_ref[0, rows, :].astype(_F32)
    glu = u[:, :C_CHANNELS] * _sigmoid(u[:, C_CHANNELS:])
    hist = jnp.concatenate([carry['tail'], glu], axis=0)
    carry['tail'] = glu[BLK - HALO:, :]
    yield
    hist_b = hist.astype(_BF16)
    acc = None
    for j0 in range(0, C_CONV_WIDTH, CONV_TAP_GROUP):
        taps = range(j0, min(j0 + CONV_TAP_GROUP, C_CONV_WIDTH))
        parts = jnp.concatenate([hist_b * dww_ref[j] for j in taps], axis=0)
        part_sum = _dot(tap_ref[:, j0 * HIST_ROWS:(taps[-1] + 1) * HIST_ROWS], parts)
        acc = part_sum if acc is None else acc + part_sum
        yield
    hcv = acc + dwb_ref[...]
    mu = jnp.mean(hcv, axis=-1, keepdims=True)
    ctr = hcv - mu
    yield
    var = jnp.mean(ctr * ctr, axis=-1, keepdims=True)
    hn = ctr * lax.rsqrt(var + EPS) * lng_ref[...] + lnb_ref[...]
    return _silu(hn)


def _interleave(*gens):
    results = [None] * len(gens)
    active = list(enumerate(gens))
    while active:
        still = []
        for k, g in active:
            try:
                next(g)
                still.append((k, g))
            except StopIteration as stop:
                results[k] = stop.value
        active = still
        yield
    return results


def _drain(gen):
    try:
        while True:
            next(gen)
    except StopIteration as stop:
        return stop.value


def _mixer_tile(i, pf_ref, pa_ref, pb_ref, pbp_ref, pbm_ref, pc_ref, pcp_ref, lbc_ref, gain_ref, sink_ref,
                dww_ref, dwb_ref, lng_ref, lnb_ref, cdec_ref, lvl_ref, avg_ref, bias_ref, tap_ref,
                y_ref, st_ref, r_ref):
    u_prev = pcp_ref[0, BLK - HALO:BLK, :].astype(_F32)
    tail = u_prev[:, :C_CHANNELS] * _sigmoid(u_prev[:, C_CHANNELS:])
    carry = {'st': [st_ref[p] for p in range(A_HEADS // 2)],
             'kv': pbp_ref[0, :, B_WIDTH:B_WIDTH + 2 * B_KV_WIDTH],
             'tail': jnp.where(i > 0, tail, 0.0)}
    gens = []
    for blk in range(TILE_BLOCKS):
        rows = slice(blk * BLK, (blk + 1) * BLK)
        n = i * TILE_BLOCKS + blk
        gens += [_hgrn_block(rows, pf_ref, pa_ref, lbc_ref, gain_ref, cdec_ref, lvl_ref, avg_ref, carry,
                             r_ref.at[blk]),
                 _attn_block(rows, n, pb_ref, bias_ref, sink_ref, carry, pbm_ref),
                 _conv_block(rows, pc_ref, dww_ref, dwb_ref, lng_ref, lnb_ref, tap_ref, carry)]
    res = yield from _interleave(*gens)
    for blk in range(TILE_BLOCKS):
        rows = slice(blk * BLK, (blk + 1) * BLK)
        y = jnp.concatenate(res[3 * blk:3 * blk + 3], axis=1)
        if blk == 0:
            row = lax.broadcasted_iota(jnp.int32, y.shape, 0)
            y = jnp.where((i > 0) | (row >= PAD), y, 0.0)
        y_ref[0, rows, :] = y.astype(_BF16)
    for p in range(A_HEADS // 2):
        st_ref[p] = carry['st'][p]


def _mixer_kernel(pf_ref, pa_ref, pb_ref, pbp_ref, pbm_ref, pc_ref, pcp_ref, lbc_ref, gain_ref, sink_ref,
                  dww_ref, dwb_ref, lng_ref, lnb_ref, cdec_ref, lvl_ref, avg_ref, bias_ref, tap_ref,
                  y_ref, st_ref, r_ref):
    i = pl.program_id(1)

    @pl.when(i == 0)
    def _():
        st_ref[...] = jnp.zeros_like(st_ref)

    _drain(_mixer_tile(i, pf_ref, pa_ref, pb_ref, pbp_ref, pbm_ref, pc_ref, pcp_ref, lbc_ref, gain_ref,
                       sink_ref, dww_ref, dwb_ref, lng_ref, lnb_ref, cdec_ref, lvl_ref, avg_ref, bias_ref,
                       tap_ref, y_ref, st_ref, r_ref))


def _mixer_call(pf, pa, pb, pc, lbc, gain, sinks, dww, dwb, lng, lnb):
    bsz, lp, _ = pf.shape
    tile = TILE_BLOCKS * BLK
    assert lp % tile == 0
    cur = lambda w: pl.BlockSpec((1, tile, w), lambda b, i: (b, i, 0))
    prev = lambda w: pl.BlockSpec((1, BLK, w), lambda b, i: (b, jnp.maximum(i * TILE_BLOCKS - 1, 0), 0))
    first = lambda w: pl.BlockSpec((1, BLK, w), lambda b, i: (b, 0, 0))
    const = lambda *shape: pl.BlockSpec(shape, lambda b, i: (0,) * len(shape))
    cdec = jnp.asarray(_decay_matrix(), _BF16)
    lvl = jnp.asarray(_level_matrix(), _BF16)
    avg = jnp.asarray(_head_mean_matrix(), _BF16)
    bias = jnp.asarray(_attn_bias())
    tap = jnp.asarray(_conv_tap_matrix(), _BF16)
    dww = jnp.broadcast_to(dww.astype(_BF16)[:, None, :], (C_CONV_WIDTH, HIST_ROWS, C_CHANNELS))
    return pl.pallas_call(
        _mixer_kernel,
        grid=(bsz, lp // tile),
        in_specs=[cur(A_WIDTH), cur(PA_WIDTH), cur(PB_WIDTH), prev(PB_WIDTH), first(PB_WIDTH),
                  cur(PC_WIDTH), prev(PC_WIDTH),
                  const(3, A_WIDTH), const(1, A_WIDTH), pl.BlockSpec(memory_space=pltpu.SMEM),
                  const(C_CONV_WIDTH, HIST_ROWS, C_CHANNELS), const(1, C_CHANNELS), const(1, C_CHANNELS),
                  const(1, C_CHANNELS),
                  const(*cdec.shape), const(*lvl.shape), const(*avg.shape), const(*bias.shape),
                  const(*tap.shape)],
        out_specs=pl.BlockSpec((1, tile, D_MODEL), lambda b, i: (b, i, 0)),
        out_shape=jax.ShapeDtypeStruct((bsz, lp, D_MODEL), _BF16),
        scratch_shapes=[pltpu.VMEM((A_HEADS // 2, LANE, LANE), _F32),
                        pltpu.VMEM((TILE_BLOCKS, N_GROUPS * BLK, A_WIDTH), _F32)],
        compiler_params=pltpu.CompilerParams(dimension_semantics=("parallel", "arbitrary"),
                                             vmem_limit_bytes=VMEM_LIMIT_BYTES),
        name="mixer",
    )(pf, pa, pb, pb, pb, pc, pc, lbc, gain, sinks, dww, dwb, lng, lnb, cdec, lvl, avg, bias, tap)


def _bq_ranges(base):
    n_col = B_Q_HEADS // B_KV_HEADS
    heads = [c + n_col * j for c in range(n_col) for j in range(B_KV_HEADS)]
    return [(base + h * B_HEAD_DIM, base + (h + 1) * B_HEAD_DIM) for h in heads]


def _win_ranges():
    b0 = 4 * A_WIDTH
    bk0 = b0 + B_WIDTH
    return ([(A_WIDTH, 2 * A_WIDTH), (0, A_WIDTH), (2 * A_WIDTH, 4 * A_WIDTH)] + _bq_ranges(b0)
            + [(bk0, IN_WIDTH)])


def _wout_ranges():
    return [(0, A_WIDTH)] + _bq_ranges(A_WIDTH) + [(A_WIDTH + B_WIDTH, D_MODEL)]


def _take_ranges(a, ranges, axis):
    return jnp.concatenate([lax.slice_in_dim(a, s, e, axis=axis) for s, e in ranges], axis=axis)


def kernel(x, meta_tokens, ffn1_norm, ffn1_w_gate, ffn1_w_up, ffn1_w_down, mix_norm, w_in, w_out,
           hgrn_lb_logits, hgrn_out_norm, attn_sinks, conv_dw_w, conv_dw_b, conv_ln_g, conv_ln_b,
           ffn2_norm, ffn2_w_gate, ffn2_w_up, ffn2_w_down, final_norm):
    bsz, seq, _ = x.shape
    depth = w_in.shape[0]
    lp = PAD + N_META + seq
    assert lp % (TILE_BLOCKS * BLK) == 0

    head = jnp.concatenate([jnp.zeros((PAD, D_MODEL), _F32), meta_tokens.astype(_F32)], axis=0)
    x_blocks = x.astype(_F32).reshape(bsz, seq // BLK, BLK, D_MODEL)

    lbs = jnp.cumsum(jax.nn.softmax(hgrn_lb_logits.astype(_F32), axis=0), axis=0)
    lbs = lbs - lbs[0]
    lbc = jnp.stack([jnp.log(lbs), jnp.log1p(-lbs), 1.0 - lbs], axis=1)
    gain = jnp.tile(hgrn_out_norm.astype(_F32), (1, A_HEADS))[:, None, :]
    sinks = attn_sinks.astype(_F32)

    row2 = lambda a: a.astype(_F32)[None, :]
    win_all = jnp.concatenate(
        [(lax.slice_in_dim(w_in, s, e, axis=2) * (B_HEAD_DIM ** -0.5 if (s, e) in _bq_ranges(4 * A_WIDTH) else 1.0)
          ).astype(_BF16) for s, e in _win_ranges()], axis=2)
    wout_all = _take_ranges(w_out.astype(_BF16), _wout_ranges(), 1)
    w1 = (ffn1_w_gate.astype(_BF16), ffn1_w_up.astype(_BF16), ffn1_w_down.astype(_BF16))
    w2 = (ffn2_w_gate.astype(_BF16), ffn2_w_up.astype(_BF16), ffn2_w_down.astype(_BF16))

    tm = 768 if (bsz * lp) % 768 == 0 else BLK
    for l in range(depth):
        if l == 0:
            h, pf, pa, pb, pc = _first_ffn_call(x_blocks, head, l, row2(ffn1_norm[l]), *w1,
                                                (row2(mix_norm[l]), win_all))
        else:
            h, pf, pa, pb, pc = _ffn_call(h, l, row2(ffn1_norm[l]), *w1, post=(row2(mix_norm[l]), win_all),
                                          tm=tm)
        shp = lambda a: a.reshape(bsz, lp, a.shape[-1])
        y = _mixer_call(shp(pf), shp(pa), shp(pb), shp(pc), lbc[l], gain[l], sinks[l],
                        conv_dw_w[l].astype(_F32), row2(conv_dw_b[l]), row2(conv_ln_g[l]),
                        row2(conv_ln_b[l]))
        if l == depth - 1:
            blocks = lambda a: a.reshape(bsz, lp // BLK, BLK, D_MODEL)
            return _final_ffn_call(blocks(h), l, row2(ffn2_norm[l]), *w2, (blocks(y), wout_all),
                                   row2(final_norm))
        h = _ffn_call(h, l, row2(ffn2_norm[l]), *w2, pre=(y.reshape(bsz * lp, D_MODEL), wout_all), tm=tm)[0]
```

```python
import functools

import numpy as np
import jax
import jax.numpy as jnp
from jax import lax
from jax.experimental import pallas as pl
from jax.experimental.pallas import tpu as pltpu

D_MODEL = 1024
D_FF = 2816
N_META = 16
EPS = 1e-6
LOG2_E = 1.4426950408889634
FFN_HALF = 0.5

A_HEADS = 6
A_DK = 64
A_WIDTH = 384
B_HEAD_DIM = 64
B_WIDTH = 384
B_Q_HEADS = 6
B_KV_HEADS = 2
B_KV_WIDTH = B_KV_HEADS * B_HEAD_DIM
B_WINDOW = 128
C_CHANNELS = 256
C_CONV_WIDTH = 31
IN_WIDTH = 2688

BLK = 128
PAD = BLK - N_META
LANE = 128
HALO = 32

PA_WIDTH = 3 * A_WIDTH
PB_WIDTH = B_WIDTH + 2 * B_KV_WIDTH
PC_WIDTH = 2 * C_CHANNELS

VMEM_LIMIT_BYTES = 56 * 1024 * 1024

_F32 = jnp.float32
_BF16 = jnp.bfloat16


def _dot(a, b):
    return jnp.dot(a, b, preferred_element_type=_F32)


def _dot_nt(a, b):
    return lax.dot_general(a, b, (((1,), (1,)), ((), ())), preferred_element_type=_F32)


def _rms(x, g):
    ms = jnp.mean(x * x, axis=-1, keepdims=True)
    return x * lax.rsqrt(ms + EPS) * g


def _silu(x):
    return x * (1.0 / (1.0 + jnp.exp(-x)))


def _sigmoid(x):
    return 1.0 / (1.0 + jnp.exp(-x))


def _ffn_kernel(*refs, pre, post, final, ff_chunk, first=False):
    it = iter(refs)
    h_ref = next(it)
    if first:
        head_ref = next(it)
    if pre:
        y_ref, wout_ref = next(it), next(it)
    g_ref, wg_ref, wu_ref, wd_ref = next(it), next(it), next(it), next(it)
    if post:
        g2_ref, win_ref = next(it), next(it)
    if final:
        gf_ref = next(it)
    hout_ref = next(it)
    if post:
        pf_ref, pa_ref, pb_ref, pc_ref = next(it), next(it), next(it), next(it)
    act_ref = next(it)

    rows = hout_ref.shape[-2]
    h = h_ref[...].reshape(rows, D_MODEL)
    if first:
        shifted = jnp.concatenate([head_ref[...], h[:rows - BLK, :]], axis=0)
        h = jnp.where(pl.program_id(1) == 0, shifted, h)
    if pre:
        h = h + _dot(y_ref[...].reshape(rows, D_MODEL), wout_ref[...])
    xn = _rms(h, g_ref[...]).astype(_BF16)
    for start in range(0, D_FF, ff_chunk):
        cols = slice(start, min(start + ff_chunk, D_FF))
        gate = _dot(xn, wg_ref[:, cols])
        up = _dot(xn, wu_ref[:, cols])
        act_ref[:, cols] = (_silu(gate) * up).astype(_BF16)
    h = h + FFN_HALF * _dot(act_ref[...], wd_ref[...])
    if final:
        hout_ref[...] = _rms(h, gf_ref[...]).reshape(hout_ref.shape)
    else:
        hout_ref[...] = h
    if post:
        xn2 = _rms(h, g2_ref[...]).astype(_BF16)
        pf_ref[...] = _dot(xn2, win_ref[:, 0:A_WIDTH])
        o = A_WIDTH
        pa_ref[...] = _dot(xn2, win_ref[:, o:o + PA_WIDTH]).astype(_BF16)
        o += PA_WIDTH
        pb_ref[...] = _dot(xn2, win_ref[:, o:o + PB_WIDTH]).astype(_BF16)
        o += PB_WIDTH
        pc_ref[...] = _dot(xn2, win_ref[:, o:o + PC_WIDTH]).astype(_BF16)


def _resident(shape):
    return pl.BlockSpec(shape, lambda *_: (0,) * len(shape), pipeline_mode=pl.Buffered(1))


def _layer_weight(layer, shape):
    return pl.BlockSpec((None,) + shape, lambda *_: (layer,) + (0,) * len(shape),
                        pipeline_mode=pl.Buffered(1))


def _ffn_call(h, layer, norm_g, wg, wu, wd, *, pre=None, post=None, tm=512, ff_chunk=768):
    rows = h.shape[0]
    assert rows % tm == 0
    row_spec = lambda w: pl.BlockSpec((tm, w), lambda i: (i, 0))
    weight = lambda *shape: _layer_weight(layer, shape)
    args, specs = [h], [row_spec(D_MODEL)]
    if pre is not None:
        y, wout = pre
        args += [y, wout]
        specs += [row_spec(D_MODEL), weight(D_MODEL, D_MODEL)]
    args += [norm_g, wg, wu, wd]
    specs += [_resident((1, D_MODEL)), weight(D_MODEL, D_FF), weight(D_MODEL, D_FF), weight(D_FF, D_MODEL)]
    if post is not None:
        g2, win = post
        args += [g2, win]
        specs += [_resident((1, D_MODEL)), weight(D_MODEL, IN_WIDTH)]
    out_shape = [jax.ShapeDtypeStruct((rows, D_MODEL), _F32)]
    out_specs = [row_spec(D_MODEL)]
    if post is not None:
        for w, dt in ((A_WIDTH, _F32), (PA_WIDTH, _BF16), (PB_WIDTH, _BF16), (PC_WIDTH, _BF16)):
            out_shape.append(jax.ShapeDtypeStruct((rows, w), dt))
            out_specs.append(row_spec(w))
    kern = functools.partial(_ffn_kernel, pre=pre is not None, post=post is not None, final=False,
                             ff_chunk=ff_chunk)
    return pl.pallas_call(
        kern,
        grid=(rows // tm,),
        in_specs=specs,
        out_specs=out_specs,
        out_shape=out_shape,
        scratch_shapes=[pltpu.VMEM((tm, D_FF), _BF16)],
        compiler_params=pltpu.CompilerParams(dimension_semantics=("parallel",),
                                             vmem_limit_bytes=VMEM_LIMIT_BYTES),
        name="ffn",
    )(*args)


def _first_ffn_call(x, head, layer, norm_g, wg, wu, wd, post, ff_chunk=768):
    bsz, nbx = x.shape[0], x.shape[1]
    g = TILE_BLOCKS
    tiles = (nbx + 1) // g
    assert tiles * g == nbx + 1
    g2, win = post
    weight = lambda *shape: _layer_weight(layer, shape)
    x_tile = pl.BlockSpec((pl.Element(1), pl.Element(g), pl.Element(BLK), pl.Element(D_MODEL)),
                          lambda b, j: (b, jnp.maximum(g * j - 1, 0), 0, 0))
    row_spec = lambda w: pl.BlockSpec((g * BLK, w), lambda b, j: (b * tiles + j, 0))
    widths = ((D_MODEL, _F32), (A_WIDTH, _F32), (PA_WIDTH, _BF16), (PB_WIDTH, _BF16), (PC_WIDTH, _BF16))
    kern = functools.partial(_ffn_kernel, pre=False, post=True, final=False, ff_chunk=ff_chunk, first=True)
    return pl.pallas_call(
        kern,
        grid=(bsz, tiles),
        in_specs=[x_tile, _resident((BLK, D_MODEL)), _resident((1, D_MODEL)), weight(D_MODEL, D_FF),
                  weight(D_MODEL, D_FF), weight(D_FF, D_MODEL), _resident((1, D_MODEL)),
                  weight(D_MODEL, IN_WIDTH)],
        out_specs=[row_spec(w) for w, _ in widths],
        out_shape=[jax.ShapeDtypeStruct((bsz * tiles * g * BLK, w), dt) for w, dt in widths],
        scratch_shapes=[pltpu.VMEM((g * BLK, D_FF), _BF16)],
        compiler_params=pltpu.CompilerParams(dimension_semantics=("parallel", "parallel"),
                                             vmem_limit_bytes=VMEM_LIMIT_BYTES),
        name="ffn_first",
    )(x, head, norm_g, wg, wu, wd, g2, win)


def _final_ffn_call(h, layer, norm_g, wg, wu, wd, pre, final_g, ff_chunk=768):
    y, wout = pre
    bsz, nb = h.shape[0], h.shape[1]
    g = max(k for k in (4, 2, 1) if (nb - 1) % k == 0)
    tile = lambda: pl.BlockSpec((pl.Element(1), pl.Element(g), pl.Element(BLK), pl.Element(D_MODEL)),
                                lambda b, j: (b, 1 + g * j, 0, 0))
    weight = lambda *shape: _layer_weight(layer, shape)
    kern = functools.partial(_ffn_kernel, pre=True, post=False, final=True, ff_chunk=ff_chunk)
    return pl.pallas_call(
        kern,
        grid=(bsz, (nb - 1) // g),
        in_specs=[tile(), tile(), weight(D_MODEL, D_MODEL), _resident((1, D_MODEL)), weight(D_MODEL, D_FF),
                  weight(D_MODEL, D_FF), weight(D_FF, D_MODEL), _resident((1, D_MODEL))],
        out_specs=pl.BlockSpec((1, g * BLK, D_MODEL), lambda b, j: (b, j, 0)),
        out_shape=jax.ShapeDtypeStruct((bsz, (nb - 1) * BLK, D_MODEL), _F32),
        scratch_shapes=[pltpu.VMEM((g * BLK, D_FF), _BF16)],
        compiler_params=pltpu.CompilerParams(dimension_semantics=("parallel", "parallel"),
                                             vmem_limit_bytes=VMEM_LIMIT_BYTES),
        name="ffn_final",
    )(h, y, wout, norm_g, wg, wu, wd, final_g)


N_LEVELS = 7
MXU_LEVELS = 3
N_GROUPS = MXU_LEVELS + 1
DECAY_TERMS = 2
HIST_ROWS = HALO + BLK
CONV_TAP_GROUP = 8
TILE_BLOCKS = 3


def _split_bf16(x, terms):
    out = []
    for _ in range(terms - 1):
        part = x.astype(_BF16)
        out.append(part)
        x = x - part.astype(_F32)
    out.append(x.astype(_BF16))
    return out


def _decay_matrix():
    c = np.zeros((N_GROUPS, BLK, BLK), np.float32)
    for t in range(BLK):
        c[0, t, :t + 1] = 1.0
        for l in range(1, MXU_LEVELS + 1):
            m = 1 << (l - 1)
            ref = (t // (2 * m)) * 2 * m + m - 1
            if t > ref:
                c[l, t, ref + 1:t + 1] = 1.0
            else:
                c[l, t, t + 1:ref + 1] = 1.0
    c = c.reshape(N_GROUPS * BLK, BLK)
    return np.concatenate([c] * DECAY_TERMS, axis=1)


def _level_matrix():
    t = np.arange(BLK)[:, None]
    s = np.arange(BLK)[None, :]
    x = t ^ s
    lvl = np.where(x == 0, 0, np.floor(np.log2(np.maximum(x, 1))) + 1)
    lvl = np.where(s > t, -1, lvl).astype(np.float32)
    return np.concatenate([lvl, lvl], axis=1)


def _head_mean_matrix():
    h = np.arange(A_WIDTH) // A_DK
    return np.where(h[:, None] == h[None, :], 1.0 / A_DK, 0.0).astype(np.float32)


def _conv_tap_matrix():
    first = HALO - (C_CONV_WIDTH - 1)
    m = np.zeros((BLK, C_CONV_WIDTH, HIST_ROWS), np.float32)
    for t in range(BLK):
        for j in range(C_CONV_WIDTH):
            m[t, j, t + first + j] = 1.0
    return m.reshape(BLK, C_CONV_WIDTH * HIST_ROWS)


def _attn_bias():
    n_keys = 2 * BLK + N_META
    qi = np.arange(BLK)[:, None]
    kj = np.arange(n_keys)[None, :]
    out = []
    for n in range(3):
        prev_ok = (kj < BLK) & (kj > qi) & (n >= 2)
        cur_ok = (kj >= BLK) & (kj < 2 * BLK) & (kj - BLK <= qi) & (n >= 1)
        meta_ok = (kj >= 2 * BLK) & ((n >= 1) | (kj - 2 * BLK + PAD <= qi))
        out.append(np.where(prev_ok | cur_ok | meta_ok, 0.0, -1e30).T)
    return np.stack(out).astype(np.float32)


def _level_decay(g_p, m):
    ref = jnp.concatenate([jnp.broadcast_to(g_p[r:r + 1, :], (2 * m, g_p.shape[1]))
                           for r in range(m - 1, BLK, 2 * m)], axis=0)
    return -jnp.abs(g_p - ref)


def _hgrn_block(rows, pf_ref, pa_ref, lbc_ref, gain_ref, cdec_ref, lvl_ref, avg_ref, carry, r_ref):
    row = lax.broadcasted_iota(jnp.int32, (BLK, LANE), 0)
    col = lax.broadcasted_iota(jnp.int32, (BLK, LANE), 1)
    same_head = (row < A_DK) == (col < A_DK)
    lane1 = lax.broadcasted_iota(jnp.int32, (1, LANE), 1)
    lo_f = (lane1 < A_DK).astype(_F32)
    hi_f = 1.0 - lo_f
    lo_b, hi_b = lo_f.astype(_BF16), hi_f.astype(_BF16)

    z = pf_ref[0, rows, :]
    log_lb, log_1m_lb, one_m_lb = lbc_ref[0:1, :], lbc_ref[1:2, :], lbc_ref[2:3, :]
    soft = jnp.log(1.0 + jnp.exp(-jnp.abs(z)))
    log_sig = jnp.minimum(z, 0.0) - soft
    log_sig_neg = jnp.minimum(-z, 0.0) - soft
    b = log_1m_lb + log_sig
    log_f = jnp.maximum(log_lb, b) + jnp.log(1.0 + jnp.exp(-jnp.abs(log_lb - b)))
    kk = one_m_lb * jnp.exp(log_sig_neg)
    kk_b = kk.astype(_BF16)
    yield

    log_f_terms = jnp.concatenate(_split_bf16(log_f * LOG2_E, DECAY_TERMS), axis=0)
    half_rows = N_GROUPS * BLK // 2
    r_ref[0:half_rows, :] = _dot(cdec_ref[0:half_rows, :], log_f_terms)
    r_ref[half_rows:, :] = _dot(cdec_ref[half_rows:, :], log_f_terms)
    grp = lambda g, lanes: r_ref[g * BLK:(g + 1) * BLK, lanes]
    decay_tot = jnp.exp2(r_ref[BLK - 1:BLK, :])
    lvl = lvl_ref[...]
    yield

    def pair(p):
        lanes = slice(p * LANE, (p + 1) * LANE)
        q_b = pa_ref[0, rows, lanes]
        g_p = grp(0, lanes)
        v_b = pa_ref[0, rows, A_WIDTH + p * LANE:A_WIDTH + (p + 1) * LANE]
        k_both = jnp.concatenate([(kk[:, lanes] * lo_f).astype(_BF16),
                                  (kk[:, lanes] * hi_f).astype(_BF16)], axis=0)
        a_cat = jnp.where(lvl == 0.0, _dot_nt(q_b, k_both).astype(_BF16),
                          jnp.zeros((BLK, 2 * BLK), _BF16))
        yield
        for l in range(1, N_LEVELS + 1):
            e = jnp.exp2(grp(l, lanes) if l <= MXU_LEVELS else _level_decay(g_p, 1 << (l - 1))).astype(_BF16)
            k_cat = k_both * jnp.concatenate([e, e], axis=0)
            a_cat = jnp.where(lvl == float(l), _dot_nt(q_b * e, k_cat).astype(_BF16), a_cat)
            yield
        v_cat = jnp.concatenate([v_b * lo_b, v_b * hi_b], axis=0)
        o_intra = _dot(a_cat, v_cat)
        yield

        st = carry['st'][p]
        qg = q_b * jnp.exp2(g_p).astype(_BF16)
        o_inter = _dot_nt(qg, st.astype(_BF16))
        kdec = kk_b[:, lanes] * jnp.exp2(g_p[BLK - 1:BLK, :] - g_p).astype(_BF16)
        upd = lax.dot_general(v_b, kdec, (((0,), (0,)), ((), ())), preferred_element_type=_F32)
        carry['st'][p] = st * decay_tot[:, lanes] + jnp.where(same_head, upd, 0.0)
        return o_inter + o_intra

    outs = yield from _interleave(*[pair(p) for p in range(A_HEADS // 2)])
    o = jnp.concatenate(outs, axis=1)

    ms = _dot((o * o).astype(_BF16), avg_ref[...])
    gate = pa_ref[0, rows, 2 * A_WIDTH:3 * A_WIDTH].astype(_F32)
    return o * lax.rsqrt(ms + EPS) * gain_ref[...] * _silu(gate)


def _attn_block(rows, n, pb_ref, bias_ref, sink_ref, carry, pbm_ref):
    kv_cur = pb_ref[0, rows, B_WIDTH:B_WIDTH + 2 * B_KV_WIDTH]

    kv_meta = pbm_ref[0, PAD:BLK, B_WIDTH:B_WIDTH + 2 * B_KV_WIDTH]
    kv_all = jnp.concatenate([carry['kv'], kv_cur, kv_meta], axis=0)
    carry['kv'] = kv_cur
    k_all, v_all = kv_all[:, :B_KV_WIDTH], kv_all[:, B_KV_WIDTH:]
    n_keys = 2 * BLK + N_META

    lane1 = lax.broadcasted_iota(jnp.int32, (1, LANE), 1)
    lo_b = (lane1 < B_HEAD_DIM).astype(_BF16)
    hi_b = (lane1 >= B_HEAD_DIM).astype(_BF16)
    n_col = B_Q_HEADS // B_KV_HEADS
    q_cat = jnp.concatenate([pb_ref[0, rows, c * LANE:(c + 1) * LANE] * (lo_b if j == 0 else hi_b)
                             for j in range(B_KV_HEADS) for c in range(n_col)], axis=0)
    sink = jnp.concatenate([jnp.full((1, BLK), sink_ref[h], _F32) for h in range(B_Q_HEADS)], axis=1)
    yield
    bias = bias_ref[jnp.minimum(n, 2)]
    s = _dot_nt(k_all, q_cat) + jnp.concatenate([bias] * B_Q_HEADS, axis=1)
    yield
    mx = jnp.maximum(jnp.max(s, axis=0, keepdims=True), sink)
    yield
    pr = jnp.exp(s - mx).astype(_BF16)
    yield
    num = lax.dot_general(v_all, pr, (((0,), (0,)), ((), ())), preferred_element_type=_F32)
    den = _dot(jnp.ones((16, n_keys), _BF16), pr)[0:1, :] + jnp.exp(sink - mx)
    yield
    o = num / den
    yield
    half = B_HEAD_DIM
    outs = [jnp.concatenate([o[:half, c * BLK:(c + 1) * BLK],
                             o[half:, (n_col + c) * BLK:(n_col + c + 1) * BLK]], axis=0).T
            for c in range(n_col)]
    return jnp.concatenate(outs, axis=1)


def _conv_block(rows, pc_ref, dww_ref, dwb_ref, lng_ref, lnb_ref, tap_ref, carry):
    u = pc_ref[0, rows, :].astype(_F32)
    glu = u[:, :C_CHANNELS] * _sigmoid(u[:, C_CHANNELS:])
    hist = jnp.concatenate([carry['tail'], glu], axis=0)
    carry['tail'] = glu[BLK - HALO:, :]
    yield
    hist_b = hist.astype(_BF16)
    acc = None
    for j0 in range(0, C_CONV_WIDTH, CONV_TAP_GROUP):
        taps = range(j0, min(j0 + CONV_TAP_GROUP, C_CONV_WIDTH))
        parts = jnp.concatenate([hist_b * dww_ref[j] for j in taps], axis=0)
        part_sum = _dot(tap_ref[:, j0 * HIST_ROWS:(taps[-1] + 1) * HIST_ROWS], parts)
        acc = part_sum if acc is None else acc + part_sum
        yield
    hcv = acc + dwb_ref[...]
    mu = jnp.mean(hcv, axis=-1, keepdims=True)
    ctr = hcv - mu
    yield
    var = jnp.mean(ctr * ctr, axis=-1, keepdims=True)
    hn = ctr * lax.rsqrt(var + EPS) * lng_ref[...] + lnb_ref[...]
    return _silu(hn)


def _interleave(*gens):
    results = [None] * len(gens)
    active = list(enumerate(gens))
    while active:
        still = []
        for k, g in active:
            try:
                next(g)
                still.append((k, g))
            except StopIteration as stop:
                results[k] = stop.value
        active = still
        yield
    return results


def _paced(gen, stages):
    while True:
        for _ in range(stages):
            try:
                next(gen)
            except StopIteration as stop:
                return stop.value
        yield


def _drain(gen):
    try:
        while True:
            next(gen)
    except StopIteration as stop:
        return stop.value


def _mixer_tile(i, pf_ref, pa_ref, pb_ref, pbp_ref, pbm_ref, pc_ref, pcp_ref, lbc_ref, gain_ref, sink_ref,
                dww_ref, dwb_ref, lng_ref, lnb_ref, cdec_ref, lvl_ref, avg_ref, bias_ref, tap_ref,
                y_ref, st_ref, r_ref):
    u_prev = pcp_ref[0, BLK - HALO:BLK, :].astype(_F32)
    tail = u_prev[:, :C_CHANNELS] * _sigmoid(u_prev[:, C_CHANNELS:])
    carry = {'st': [st_ref[p] for p in range(A_HEADS // 2)],
             'kv': pbp_ref[0, :, B_WIDTH:B_WIDTH + 2 * B_KV_WIDTH],
             'tail': jnp.where(i > 0, tail, 0.0)}
    gens = []
    for blk in range(TILE_BLOCKS):
        rows = slice(blk * BLK, (blk + 1) * BLK)
        n = i * TILE_BLOCKS + blk
        gens += [_paced(_hgrn_block(rows, pf_ref, pa_ref, lbc_ref, gain_ref, cdec_ref, lvl_ref, avg_ref, carry,
                                    r_ref.at[blk]), 2),
                 _attn_block(rows, n, pb_ref, bias_ref, sink_ref, carry, pbm_ref),
                 _conv_block(rows, pc_ref, dww_ref, dwb_ref, lng_ref, lnb_ref, tap_ref, carry)]
    res = yield from _interleave(*gens)
    for blk in range(TILE_BLOCKS):
        rows = slice(blk * BLK, (blk + 1) * BLK)
        y = jnp.concatenate(res[3 * blk:3 * blk + 3], axis=1)
        if blk == 0:
            row = lax.broadcasted_iota(jnp.int32, y.shape, 0)
            y = jnp.where((i > 0) | (row >= PAD), y, 0.0)
        y_ref[0, rows, :] = y.astype(_BF16)
    for p in range(A_HEADS // 2):
        st_ref[p] = carry['st'][p]


def _mixer_kernel(pf_ref, pa_ref, pb_ref, pbp_ref, pbm_ref, pc_ref, pcp_ref, lbc_ref, gain_ref, sink_ref,
                  dww_ref, dwb_ref, lng_ref, lnb_ref, cdec_ref, lvl_ref, avg_ref, bias_ref, tap_ref,
                  y_ref, st_ref, r_ref):
    i = pl.program_id(1)

    @pl.when(i == 0)
    def _():
        st_ref[...] = jnp.zeros_like(st_ref)

    _drain(_mixer_tile(i, pf_ref, pa_ref, pb_ref, pbp_ref, pbm_ref, pc_ref, pcp_ref, lbc_ref, gain_ref,
                       sink_ref, dww_ref, dwb_ref, lng_ref, lnb_ref, cdec_ref, lvl_ref, avg_ref, bias_ref,
                       tap_ref, y_ref, st_ref, r_ref))


def _mixer_call(pf, pa, pb, pc, lbc, gain, sinks, dww, dwb, lng, lnb):
    bsz, lp, _ = pf.shape
    tile = TILE_BLOCKS * BLK
    assert lp % tile == 0
    cur = lambda w: pl.BlockSpec((1, tile, w), lambda b, i: (b, i, 0))
    prev = lambda w: pl.BlockSpec((1, BLK, w), lambda b, i: (b, jnp.maximum(i * TILE_BLOCKS - 1, 0), 0))
    first = lambda w: pl.BlockSpec((1, BLK, w), lambda b, i: (b, 0, 0))
    const = lambda *shape: pl.BlockSpec(shape, lambda b, i: (0,) * len(shape))
    cdec = jnp.asarray(_decay_matrix(), _BF16)
    lvl = jnp.asarray(_level_matrix(), _BF16)
    avg = jnp.asarray(_head_mean_matrix(), _BF16)
    bias = jnp.asarray(_attn_bias())
    tap = jnp.asarray(_conv_tap_matrix(), _BF16)
    dww = jnp.broadcast_to(dww.astype(_BF16)[:, None, :], (C_CONV_WIDTH, HIST_ROWS, C_CHANNELS))
    return pl.pallas_call(
        _mixer_kernel,
        grid=(bsz, lp // tile),
        in_specs=[cur(A_WIDTH), cur(PA_WIDTH), cur(PB_WIDTH), prev(PB_WIDTH), first(PB_WIDTH),
                  cur(PC_WIDTH), prev(PC_WIDTH),
                  const(3, A_WIDTH), const(1, A_WIDTH), pl.BlockSpec(memory_space=pltpu.SMEM),
                  const(C_CONV_WIDTH, HIST_ROWS, C_CHANNELS), const(1, C_CHANNELS), const(1, C_CHANNELS),
                  const(1, C_CHANNELS),
                  const(*cdec.shape), const(*lvl.shape), const(*avg.shape), const(*bias.shape),
                  const(*tap.shape)],
        out_specs=pl.BlockSpec((1, tile, D_MODEL), lambda b, i: (b, i, 0)),
        out_shape=jax.ShapeDtypeStruct((bsz, lp, D_MODEL), _BF16),
        scratch_shapes=[pltpu.VMEM((A_HEADS // 2, LANE, LANE), _F32),
                        pltpu.VMEM((TILE_BLOCKS, N_GROUPS * BLK, A_WIDTH), _F32)],
        compiler_params=pltpu.CompilerParams(dimension_semantics=("parallel", "arbitrary"),
                                             vmem_limit_bytes=VMEM_LIMIT_BYTES),
        name="mixer",
    )(pf, pa, pb, pb, pb, pc, pc, lbc, gain, sinks, dww, dwb, lng, lnb, cdec, lvl, avg, bias, tap)


def _bq_ranges(base):
    n_col = B_Q_HEADS // B_KV_HEADS
    heads = [c + n_col * j for c in range(n_col) for j in range(B_KV_HEADS)]
    return [(base + h * B_HEAD_DIM, base + (h + 1) * B_HEAD_DIM) for h in heads]


def _win_ranges():
    b0 = 4 * A_WIDTH
    bk0 = b0 + B_WIDTH
    return ([(A_WIDTH, 2 * A_WIDTH), (0, A_WIDTH), (2 * A_WIDTH, 4 * A_WIDTH)] + _bq_ranges(b0)
            + [(bk0, IN_WIDTH)])


def _wout_ranges():
    return [(0, A_WIDTH)] + _bq_ranges(A_WIDTH) + [(A_WIDTH + B_WIDTH, D_MODEL)]


def _take_ranges(a, ranges, axis):
    return jnp.concatenate([lax.slice_in_dim(a, s, e, axis=axis) for s, e in ranges], axis=axis)


def kernel(x, meta_tokens, ffn1_norm, ffn1_w_gate, ffn1_w_up, ffn1_w_down, mix_norm, w_in, w_out,
           hgrn_lb_logits, hgrn_out_norm, attn_sinks, conv_dw_w, conv_dw_b, conv_ln_g, conv_ln_b,
           ffn2_norm, ffn2_w_gate, ffn2_w_up, ffn2_w_down, final_norm):
    bsz, seq, _ = x.shape
    depth = w_in.shape[0]
    lp = PAD + N_META + seq
    assert lp % (TILE_BLOCKS * BLK) == 0

    head = jnp.concatenate([jnp.zeros((PAD, D_MODEL), _F32), meta_tokens.astype(_F32)], axis=0)
    x_blocks = x.astype(_F32).reshape(bsz, seq // BLK, BLK, D_MODEL)

    lbs = jnp.cumsum(jax.nn.softmax(hgrn_lb_logits.astype(_F32), axis=0), axis=0)
    lbs = lbs - lbs[0]
    lbc = jnp.stack([jnp.log(lbs), jnp.log1p(-lbs), 1.0 - lbs], axis=1)
    gain = jnp.tile(hgrn_out_norm.astype(_F32), (1, A_HEADS))[:, None, :]
    sinks = attn_sinks.astype(_F32)

    row2 = lambda a: a.astype(_F32)[None, :]
    win_all = jnp.concatenate(
        [(lax.slice_in_dim(w_in, s, e, axis=2) * (B_HEAD_DIM ** -0.5 if (s, e) in _bq_ranges(4 * A_WIDTH) else 1.0)
          ).astype(_BF16) for s, e in _win_ranges()], axis=2)
    wout_all = _take_ranges(w_out.astype(_BF16), _wout_ranges(), 1)
    w1 = (ffn1_w_gate.astype(_BF16), ffn1_w_up.astype(_BF16), ffn1_w_down.astype(_BF16))
    w2 = (ffn2_w_gate.astype(_BF16), ffn2_w_up.astype(_BF16), ffn2_w_down.astype(_BF16))

    tm = 768 if (bsz * lp) % 768 == 0 else BLK
    for l in range(depth):
        if l == 0:
            h, pf, pa, pb, pc = _first_ffn_call(x_blocks, head, l, row2(ffn1_norm[l]), *w1,
                                                (row2(mix_norm[l]), win_all))
        else:
            h, pf, pa, pb, pc = _ffn_call(h, l, row2(ffn1_norm[l]), *w1, post=(row2(mix_norm[l]), win_all),
                                          tm=tm)
        shp = lambda a: a.reshape(bsz, lp, a.shape[-1])
        y = _mixer_call(shp(pf), shp(pa), shp(pb), shp(pc), lbc[l], gain[l], sinks[l],
                        conv_dw_w[l].astype(_F32), row2(conv_dw_b[l]), row2(conv_ln_g[l]),
                        row2(conv_ln_b[l]))
        if l == depth - 1:
            blocks = lambda a: a.reshape(bsz, lp // BLK, BLK, D_MODEL)
            return _final_ffn_call(blocks(h), l, row2(ffn2_norm[l]), *w2, (blocks(y), wout_all),
                                   row2(final_norm))
        h = _ffn_call(h, l, row2(ffn2_norm[l]), *w2, pre=(y.reshape(bsz * lp, D_MODEL), wout_all), tm=tm)[0]
```
